```python
import jax, jax.numpy as jnp
from jax import lax
import numpy as np

D_MODEL = 2048
BATCH = 4
SEQ = 2048
DEPTH = 4

CHUNK = 64
MEM_LEN = 256
HEAD_DIM = 128
QBLOCK = 128
ROPE_THETA = 10000.0
LN_EPS = 1e-5
NEG_INF = -1e30

N_HEADS_A = D_MODEL // (2 * HEAD_DIM)
N_HEADS_B = D_MODEL // (2 * HEAD_DIM)
IDX_HEADS = 16
IDX_DIM = 64
TOPK_MAX = 256
LEFT_CHUNKS = 8
BAND = (LEFT_CHUNKS + 1) * CHUNK
REL_CLIP = 128
REL_SIZE = CHUNK + REL_CLIP
N_HEADS_C = D_MODEL // HEAD_DIM
N_HEADS_X = 4
FFN_DIM = ((8 * D_MODEL // 3 + 255) // 256) * 256
ALPHA = (2.0 * DEPTH) ** 0.25
BETA = (8.0 * DEPTH) ** -0.25
N_EVEN = (DEPTH + 1) // 2
N_ODD = DEPTH // 2
EVEN_SIZES = (N_HEADS_A * HEAD_DIM, HEAD_DIM, HEAD_DIM, IDX_HEADS * IDX_DIM, IDX_DIM, IDX_HEADS,
              N_HEADS_B * HEAD_DIM, N_HEADS_B * HEAD_DIM, N_HEADS_B * HEAD_DIM)
EVEN_COLS = sum(EVEN_SIZES)
ODD_COLS = 3 * N_HEADS_C * HEAD_DIM

kernel_name = 'hybrid_streaming_encoder'


def layer_norm(x, g, b):
    xf = x.astype(jnp.float32)
    mu = jnp.mean(xf, axis=-1, keepdims=True)
    var = jnp.mean(jnp.square(xf - mu), axis=-1, keepdims=True)
    return ((xf - mu) * lax.rsqrt(var + LN_EPS) * g + b).astype(x.dtype)


def rope(x, pos):
    half = x.shape[-1] // 2
    inv_freq = ROPE_THETA ** (-jnp.arange(half, dtype=jnp.float32) / half)
    ang = pos.astype(jnp.float32)[:, None] * inv_freq[None, :]
    cos = jnp.cos(ang)[:, None, :]
    sin = jnp.sin(ang)[:, None, :]
    xf = x.astype(jnp.float32)
    x1, x2 = xf[..., :half], xf[..., half:]
    return jnp.concatenate([x1 * cos - x2 * sin, x2 * cos + x1 * sin], axis=-1).astype(x.dtype)


def split_cols(h, sizes):
    out, o = [], 0
    for n in sizes:
        out.append(h[..., o:o + n])
        o += n
    return out


def swiglu(x, w_in, w_out):
    h = x @ w_in
    a, g = h[..., :FFN_DIM], h[..., FFN_DIM:]
    return (jax.nn.silu(a) * g) @ w_out


def dsa_sparse_attention(q, k, v, qi, ki, wi):
    B, S, H, dh = q.shape
    topk = min(TOPK_MAX, S // 4)
    nb = S // QBLOCK
    key_chunk = jnp.arange(S) // CHUNK
    ki32 = ki.astype(jnp.float32)
    scale = dh ** -0.5

    def block(args):
        qb, qib, wib, bi = args
        q_chunk = (bi * QBLOCK + jnp.arange(QBLOCK)) // CHUNK
        allowed = key_chunk[None, :] <= q_chunk[:, None]
        rel = jax.nn.relu(jnp.einsum('bthd,bsd->bths', qib.astype(jnp.float32), ki32))
        score = jnp.einsum('bths,bth->bts', rel, wib.astype(jnp.float32))
        score = jnp.where(allowed[None], score, NEG_INF)
        _, idx = lax.top_k(score, topk)
        valid = jnp.take(key_chunk, idx) <= q_chunk[None, :, None]
        kg = jax.vmap(lambda kk, ii: kk[ii])(k, idx)
        vg = jax.vmap(lambda vv, ii: vv[ii])(v, idx)
        logits = jnp.einsum('bthd,btkd->bhtk', qb, kg, preferred_element_type=jnp.float32) * scale
        logits = jnp.where(valid[:, None], logits, NEG_INF)
        p = jax.nn.softmax(logits, axis=-1)
        return jnp.einsum('bhtk,btkd->bthd', p.astype(vg.dtype), vg)

    qs = q.reshape(B, nb, QBLOCK, H, dh).transpose(1, 0, 2, 3, 4)
    qis = qi.reshape(B, nb, QBLOCK, IDX_HEADS, IDX_DIM).transpose(1, 0, 2, 3, 4)
    wis = wi.reshape(B, nb, QBLOCK, IDX_HEADS).transpose(1, 0, 2, 3)
    o = lax.map(block, (qs, qis, wis, jnp.arange(nb)))
    return o.transpose(1, 0, 2, 3, 4).reshape(B, S, H, dh)


def chunk_band_attention(q, k, v, rel_bias):
    B, S, H, dh = q.shape
    nc = S // CHUNK
    pad = LEFT_CHUNKS * CHUNK
    kp = jnp.pad(k, ((0, 0), (pad, 0), (0, 0), (0, 0)))
    vp = jnp.pad(v, ((0, 0), (pad, 0), (0, 0), (0, 0)))
    dist = jnp.arange(CHUNK)[:, None] - jnp.arange(BAND)[None, :] + pad
    bias = rel_bias[:, jnp.clip(dist, -(CHUNK - 1), REL_CLIP) + (CHUNK - 1)].astype(jnp.float32)
    scale = dh ** -0.5

    def block(args):
        qb, c = args
        kb = lax.dynamic_slice_in_dim(kp, c * CHUNK, BAND, axis=1)
        vb = lax.dynamic_slice_in_dim(vp, c * CHUNK, BAND, axis=1)
        valid = (c * CHUNK - pad + jnp.arange(BAND)) >= 0
        logits = jnp.einsum('bihd,bjhd->bhij', qb, kb, preferred_element_type=jnp.float32) * scale + bias
        logits = jnp.where(valid, logits, NEG_INF)
        p = jax.nn.softmax(logits, axis=-1)
        return jnp.einsum('bhij,bjhd->bihd', p.astype(vb.dtype), vb)

    qc = q.reshape(B, nc, CHUNK, H, dh).transpose(1, 0, 2, 3, 4)
    o = lax.map(block, (qc, jnp.arange(nc)))
    return o.transpose(1, 0, 2, 3, 4).reshape(B, S, H, dh)


def stick_breaking_attention(q, k, v):
    B, S, H, dh = q.shape
    nb = S // QBLOCK
    kpos = jnp.arange(S)
    scale = dh ** -0.5

    def block(args):
        qb, bi = args
        t = bi * QBLOCK + jnp.arange(QBLOCK)
        z = jnp.einsum('bthd,bshd->bhts', qb, k, preferred_element_type=jnp.float32) * scale
        past = kpos[None, :] < t[:, None]
        log_beta = jax.nn.log_sigmoid(z)
        log_keep = jnp.where(past, jax.nn.log_sigmoid(-z), 0.0)
        after = lax.cumsum(log_keep, axis=3, reverse=True) - log_keep
        a = jnp.where(past, jnp.exp(log_beta + after), 0.0)
        return jnp.einsum('bhts,bshd->bthd', a.astype(v.dtype), v)

    qs = q.reshape(B, nb, QBLOCK, H, dh).transpose(1, 0, 2, 3, 4)
    o = lax.map(block, (qs, jnp.arange(nb)))
    return o.transpose(1, 0, 2, 3, 4).reshape(B, S, H, dh)


def even_mixer(x, w_in, w_out, rel_bias, pos):
    B, S, _ = x.shape
    q_a, k_a, v_a, q_i, k_i, w_i, q_b, k_b, v_b = split_cols(x @ w_in, EVEN_SIZES)
    q_a = rope(q_a.reshape(B, S, N_HEADS_A, HEAD_DIM), pos)
    k_a = rope(k_a.reshape(B, S, 1, HEAD_DIM), pos)[:, :, 0]
    q_i = rope(q_i.reshape(B, S, IDX_HEADS, IDX_DIM), pos)
    k_i = rope(k_i.reshape(B, S, 1, IDX_DIM), pos)[:, :, 0]
    o_a = dsa_sparse_attention(q_a, k_a, v_a, q_i, k_i, w_i)
    o_b = chunk_band_attention(q_b.reshape(B, S, N_HEADS_B, HEAD_DIM),
                               k_b.reshape(B, S, N_HEADS_B, HEAD_DIM),
                               v_b.reshape(B, S, N_HEADS_B, HEAD_DIM), rel_bias)
    o = jnp.concatenate([o_a.reshape(B, S, -1), o_b.reshape(B, S, -1)], axis=-1)
    return o @ w_out


def odd_mixer(x, w_in, w_out):
    B, S, _ = x.shape
    qkv = (x @ w_in).reshape(B, S, 3, N_HEADS_C, HEAD_DIM)
    o = stick_breaking_attention(qkv[:, :, 0], qkv[:, :, 1], qkv[:, :, 2])
    return o.reshape(B, S, -1) @ w_out


def memory_cross_attention(x, mem, w_q, w_kv, w_o):
    B, S, _ = x.shape
    M = mem.shape[1]
    q = (x @ w_q).reshape(B, S, N_HEADS_X, HEAD_DIM)
    kv = (mem @ w_kv).reshape(B, M, 2, N_HEADS_X, HEAD_DIM)
    logits = jnp.einsum('bthd,bmhd->bhtm', q, kv[:, :, 0], preferred_element_type=jnp.float32) * HEAD_DIM ** -0.5
    p = jax.nn.softmax(logits, axis=-1)
    o = jnp.einsum('bhtm,bmhd->bthd', p.astype(kv.dtype), kv[:, :, 1])
    return o.reshape(B, S, -1) @ w_o


def setup_inputs(seed: int = 0) -> dict:
    key = jax.random.key(seed)
    ks = jax.random.split(key, 14)
    f32 = jnp.float32
    D = D_MODEL

    def w(k, shape, fan_in, scale=1.0):
        return jax.random.normal(k, shape, f32) * (scale * fan_in ** -0.5)

    return {
        'x': jax.random.normal(ks[0], (BATCH, SEQ, D), f32),
        'mem': jax.random.normal(ks[1], (BATCH, MEM_LEN, D), f32),
        'ln_g': 1.0 + 0.05 * jax.random.normal(ks[2], (DEPTH, 4, D), f32),
        'ln_b': 0.02 * jax.random.normal(ks[3], (DEPTH, 4, D), f32),
        'ffn_in': w(ks[4], (DEPTH, 2, D, 2 * FFN_DIM), D),
        'ffn_out': w(ks[5], (DEPTH, 2, FFN_DIM, D), FFN_DIM, BETA),
        'xattn_q': w(ks[6], (DEPTH, D, N_HEADS_X * HEAD_DIM), D),
        'xattn_kv': w(ks[7], (DEPTH, D, 2 * N_HEADS_X * HEAD_DIM), D),
        'xattn_o': w(ks[8], (DEPTH, N_HEADS_X * HEAD_DIM, D), N_HEADS_X * HEAD_DIM, BETA),
        'even_in': w(ks[9], (N_EVEN, D, EVEN_COLS), D),
        'even_out': w(ks[10], (N_EVEN, (N_HEADS_A + N_HEADS_B) * HEAD_DIM, D), (N_HEADS_A + N_HEADS_B) * HEAD_DIM, BETA),
        'even_rel_bias': 0.2 * jax.random.normal(ks[11], (N_EVEN, N_HEADS_B, REL_SIZE), f32),
        'odd_in': w(ks[12], (N_ODD, D, ODD_COLS), D),
        'odd_out': w(ks[13], (N_ODD, N_HEADS_C * HEAD_DIM, D), N_HEADS_C * HEAD_DIM, BETA),
    }


def reference(x, mem, ln_g, ln_b, ffn_in, ffn_out, xattn_q, xattn_kv, xattn_o,
              even_in, even_out, even_rel_bias, odd_in, odd_out):
    pos = jnp.arange(x.shape[1])
    for layer in range(DEPTH):
        g, b = ln_g[layer], ln_b[layer]
        x = layer_norm(ALPHA * x + 0.5 * swiglu(x, ffn_in[layer, 0], ffn_out[layer, 0]), g[0], b[0])
        if layer % 2 == 0:
            i = layer // 2
            mix = even_mixer(x, even_in[i], even_out[i], even_rel_bias[i], pos)
        else:
            i = layer // 2
            mix = odd_mixer(x, odd_in[i], odd_out[i])
        x = layer_norm(ALPHA * x + mix, g[1], b[1])
        x = layer_norm(ALPHA * x + memory_cross_attention(x, mem, xattn_q[layer], xattn_kv[layer], xattn_o[layer]), g[2], b[2])
        x = layer_norm(ALPHA * x + 0.5 * swiglu(x, ffn_in[layer, 1], ffn_out[layer, 1]), g[3], b[3])
    return x
```

```python
import functools

import jax
import jax.numpy as jnp
from jax import lax
from jax.experimental import pallas as pl
from jax.experimental.pallas import tpu as pltpu

F32 = jnp.float32
BF16 = jnp.bfloat16

D_MODEL = 2048
DEPTH = 4
CHUNK = 64
MEM_LEN = 256
HEAD_DIM = 128
ROPE_THETA = 10000.0
LN_EPS = 1e-5
NEG_INF = -1e30
N_HEADS_A = 8
N_HEADS_B = 8
IDX_HEADS = 16
IDX_DIM = 64
TOPK_MAX = 256
LEFT_CHUNKS = 8
REL_CLIP = 128
REL_SIZE = CHUNK + REL_CLIP
N_HEADS_C = 16
N_HEADS_X = 4
FFN_DIM = ((8 * D_MODEL // 3 + 255) // 256) * 256
ALPHA = (2.0 * DEPTH) ** 0.25
SCALE = HEAD_DIM ** -0.5

LANE = 128
INT_MIN = -2147483648
MIB = 1024 * 1024

A_COLS = (N_HEADS_A + 2) * HEAD_DIM
I_COLS = IDX_HEADS * IDX_DIM + IDX_DIM + IDX_HEADS
I_PAD = ((I_COLS + LANE - 1) // LANE) * LANE
W_IDX_LANE = IDX_DIM
B_COLS = 3 * N_HEADS_B * HEAD_DIM

BAND_Q = 2 * CHUNK
BAND_KB = (LEFT_CHUNKS * CHUNK) // BAND_Q + 1
BAND_W = BAND_KB * BAND_Q


def _cparams(sem, vmem_mib):
    return pltpu.CompilerParams(dimension_semantics=sem, vmem_limit_bytes=vmem_mib * MIB)


def _dot(a, b):
    return jnp.dot(a, b, preferred_element_type=F32)


def _dot_nt(a, b):
    return lax.dot_general(a, b, (((1,), (1,)), ((), ())), preferred_element_type=F32)


def _layer_norm(y, g, b):
    mu = jnp.mean(y, axis=-1, keepdims=True)
    d = y - mu
    var = jnp.mean(d * d, axis=-1, keepdims=True)
    return d * lax.rsqrt(var + LN_EPS) * g + b


def _mm_kernel(x_ref, w_ref, o_ref):
    o_ref[...] = _dot(x_ref[...], w_ref[...]).astype(o_ref.dtype)


def _matmul(x, w, tm, tn, name):
    m, k = x.shape
    n = w.shape[1]
    return pl.pallas_call(
        _mm_kernel,
        grid=(n // tn, m // tm),
        in_specs=[pl.BlockSpec((tm, k), lambda j, i: (i, 0)),
                  pl.BlockSpec((k, tn), lambda j, i: (0, j))],
        out_specs=pl.BlockSpec((tm, tn), lambda j, i: (i, j)),
        out_shape=jax.ShapeDtypeStruct((m, n), BF16),
        compiler_params=_cparams(("parallel", "parallel"), 48),
        name=name,
    )(x, w)


def _rope_group(x, cos, sin, width):
    if width == LANE:
        partner = pltpu.roll(x, LANE // 2, 1)
    else:
        lane = lax.broadcasted_iota(jnp.int32, x.shape, 1)
        half = width // 2
        partner = jnp.where((lane & half) == 0, pltpu.roll(x, LANE - half, 1), pltpu.roll(x, half, 1))
    return x * cos + partner * sin


def _proj_rope_kernel(x_ref, w_ref, *rest, modes, n_tab, f32_tail):
    tabs = rest[:2 * n_tab]
    outs = rest[2 * n_tab:]
    acc = _dot(x_ref[...], w_ref[...])
    for g, mode in enumerate(modes):
        blk = acc[:, g * LANE:(g + 1) * LANE]
        if mode is not None:
            tid, width = mode
            blk = _rope_group(blk, tabs[2 * tid][...], tabs[2 * tid + 1][...], width)
        outs[0][:, g * LANE:(g + 1) * LANE] = blk.astype(BF16)
    if f32_tail:
        outs[1][...] = acc[:, -LANE:]


def _proj_rope(x, w, tabs, modes, seq, tm, f32_tail, name):
    m, k = x.shape
    n = w.shape[1]
    per_seq = seq // tm
    in_specs = [pl.BlockSpec((tm, k), lambda i: (i, 0)), pl.BlockSpec((k, n), lambda i: (0, 0))]
    args = [x, w]
    for cos, sin in tabs:
        in_specs += [pl.BlockSpec((tm, LANE), lambda i: (i % per_seq, 0))] * 2
        args += [cos, sin]
    out_specs = [pl.BlockSpec((tm, n), lambda i: (i, 0))]
    out_shape = [jax.ShapeDtypeStruct((m, n), BF16)]
    if f32_tail:
        out_specs.append(pl.BlockSpec((tm, LANE), lambda i: (i, 0)))
        out_shape.append(jax.ShapeDtypeStruct((m, LANE), F32))
    return pl.pallas_call(
        functools.partial(_proj_rope_kernel, modes=modes, n_tab=len(tabs), f32_tail=f32_tail),
        grid=(m // tm,),
        in_specs=in_specs,
        out_specs=out_specs,
        out_shape=out_shape,
        compiler_params=_cparams(("parallel",), 48),
        name=name,
    )(*args)


def _ffn_in_kernel(x_ref, wa_ref, wg_ref, o_ref):
    x = x_ref[...]
    a = _dot(x, wa_ref[...])
    g = _dot(x, wg_ref[...])
    o_ref[...] = (a * jax.nn.sigmoid(a) * g).astype(o_ref.dtype)


def _ffn_in(xb, w, tm, tf):
    m, k = xb.shape
    nf = FFN_DIM // tf
    return pl.pallas_call(
        _ffn_in_kernel,
        grid=(nf, m // tm),
        in_specs=[pl.BlockSpec((tm, k), lambda j, i: (i, 0)),
                  pl.BlockSpec((k, tf), lambda j, i: (0, j)),
                  pl.BlockSpec((k, tf), lambda j, i: (0, j + nf))],
        out_specs=pl.BlockSpec((tm, tf), lambda j, i: (i, j)),
        out_shape=jax.ShapeDtypeStruct((m, FFN_DIM), BF16),
        compiler_params=_cparams(("parallel", "parallel"), 48),
        name="ffn_in",
    )(xb, w, w)


def _ffn_out_kernel(h_ref, w_ref, x_ref, g_ref, b_ref, of_ref, ob_ref, acc_ref, *, nk, scale):
    k = pl.program_id(1)
    part = _dot(h_ref[...], w_ref[...])

    @pl.when(k == 0)
    def _():
        acc_ref[...] = part

    @pl.when(k > 0)
    def _():
        acc_ref[...] += part

    @pl.when(k == nk - 1)
    def _():
        y = _layer_norm(ALPHA * x_ref[...] + scale * acc_ref[...], g_ref[...], b_ref[...])
        of_ref[...] = y
        ob_ref[...] = y.astype(BF16)


def _ffn_out(h, w, xf, g, b, tm, tk, scale):
    m, kdim = h.shape
    n = w.shape[1]
    nk = kdim // tk
    row = lambda i, k: (i, 0)
    return pl.pallas_call(
        functools.partial(_ffn_out_kernel, nk=nk, scale=scale),
        grid=(m // tm, nk),
        in_specs=[pl.BlockSpec((tm, tk), lambda i, k: (i, k)),
                  pl.BlockSpec((tk, n), lambda i, k: (k, 0)),
                  pl.BlockSpec((tm, n), row),
                  pl.BlockSpec((1, n), lambda i, k: (0, 0)),
                  pl.BlockSpec((1, n), lambda i, k: (0, 0))],
        out_specs=[pl.BlockSpec((tm, n), row), pl.BlockSpec((tm, n), row)],
        out_shape=[jax.ShapeDtypeStruct((m, n), F32), jax.ShapeDtypeStruct((m, n), BF16)],
        scratch_shapes=[pltpu.VMEM((tm, n), F32)],
        compiler_params=_cparams(("parallel", "arbitrary"), 56),
        name="ffn_out_ln",
    )(h, w, xf, g, b)


def _out_ln_kernel(*refs, n_lhs):
    lhs = refs[:n_lhs]
    ws = refs[n_lhs:2 * n_lhs]
    x_ref, g_ref, b_ref, of_ref, ob_ref = refs[2 * n_lhs:]
    acc = _dot(lhs[0][...], ws[0][...])
    for l_ref, w_ref in zip(lhs[1:], ws[1:]):
        acc = acc + _dot(l_ref[...], w_ref[...])
    y = _layer_norm(ALPHA * x_ref[...] + acc, g_ref[...], b_ref[...])
    of_ref[...] = y
    ob_ref[...] = y.astype(BF16)


def _out_ln(lhs_list, w, xf, g, b, tm, name):
    m, n = xf.shape
    row = lambda i: (i, 0)
    in_specs, args, off = [], [], 0
    for l in lhs_list:
        in_specs.append(pl.BlockSpec((tm, l.shape[1]), row))
        args.append(l)
    for l in lhs_list:
        kl = l.shape[1]
        in_specs.append(pl.BlockSpec((kl, n), functools.partial(lambda i, o: (o, 0), o=off // kl)))
        args.append(w)
        off += kl
    in_specs += [pl.BlockSpec((tm, n), row), pl.BlockSpec((1, n), lambda i: (0, 0)), pl.BlockSpec((1, n), lambda i: (0, 0))]
    args += [xf, g, b]
    return pl.pallas_call(
        functools.partial(_out_ln_kernel, n_lhs=len(lhs_list)),
        grid=(m // tm,),
        in_specs=in_specs,
        out_specs=[pl.BlockSpec((tm, n), row), pl.BlockSpec((tm, n), row)],
        out_shape=[jax.ShapeDtypeStruct((m, n), F32), jax.ShapeDtypeStruct((m, n), BF16)],
        compiler_params=_cparams(("parallel",), 56),
        name=name,
    )(*args)


def _dsa_kernel(q_ref, k_ref, v_ref, qi_ref, ki_ref, w_ref, o_ref, key_ref, *, tq, seq, topk):
    i = pl.program_id(1)
    ki = ki_ref[:, :IDX_DIM]
    w = w_ref[...]
    score = jnp.zeros((tq, seq), F32)
    for h in range(IDX_HEADS):
        rel = jnp.maximum(_dot_nt(qi_ref[:, h * IDX_DIM:(h + 1) * IDX_DIM], ki), 0.0)
        score = score + rel * w[:, W_IDX_LANE + h:W_IDX_LANE + h + 1]
    row = lax.broadcasted_iota(jnp.int32, (tq, seq), 0)
    col = lax.broadcasted_iota(jnp.int32, (tq, seq), 1)
    allowed = (col >> 6) <= ((i * tq + row) >> 6)
    score = jnp.where(allowed, score, NEG_INF)
    bits = lax.bitcast_convert_type(score, jnp.int32)
    key = bits ^ ((bits >> 31) & 0x7FFFFFFF)
    key_ref[...] = key

    def select_bit(j, t):
        cand = t | lax.shift_left(jnp.int32(1), 31 - j)
        cnt = jnp.sum(jnp.where(key_ref[...] >= (cand ^ INT_MIN), 1.0, 0.0), axis=1, keepdims=True)
        return jnp.where(cnt >= topk, cand, t)

    t = lax.fori_loop(0, 32, select_bit, jnp.zeros((tq, 1), jnp.int32))
    thr = jnp.maximum(t ^ INT_MIN, INT_MIN + 1)
    key_ref[...] = jnp.where(allowed, key, INT_MIN)

    k = k_ref[...]
    v = v_ref[...]
    for h in range(N_HEADS_A):
        cols = slice(h * HEAD_DIM, (h + 1) * HEAD_DIM)
        lg = _dot_nt(q_ref[:, cols], k) * SCALE
        lg = jnp.where(key_ref[...] >= thr, lg, NEG_INF)
        m = jnp.max(lg, axis=1, keepdims=True)
        e = jnp.exp(lg - m)
        s = jnp.sum(e, axis=1, keepdims=True)
        o_ref[:, cols] = (_dot(e.astype(BF16), v) / s).astype(BF16)


def _dsa_attention(qkv_a, qk_i, w_i, batch, seq):
    tq = 2 * CHUNK
    nq = seq // tq
    hq = N_HEADS_A * HEAD_DIM
    kcol = hq // LANE
    icol = (IDX_HEADS * IDX_DIM) // LANE
    qmap = lambda b, i: (b * nq + i, 0)
    return pl.pallas_call(
        functools.partial(_dsa_kernel, tq=tq, seq=seq, topk=min(TOPK_MAX, seq // 4)),
        grid=(batch, nq),
        in_specs=[pl.BlockSpec((tq, hq), qmap),
                  pl.BlockSpec((seq, LANE), lambda b, i: (b, kcol)),
                  pl.BlockSpec((seq, LANE), lambda b, i: (b, kcol + 1)),
                  pl.BlockSpec((tq, IDX_HEADS * IDX_DIM), qmap),
                  pl.BlockSpec((seq, LANE), lambda b, i: (b, icol)),
                  pl.BlockSpec((tq, LANE), qmap)],
        out_specs=pl.BlockSpec((tq, hq), qmap),
        out_shape=jax.ShapeDtypeStruct((batch * seq, hq), BF16),
        scratch_shapes=[pltpu.VMEM((tq, seq), jnp.int32)],
        compiler_params=_cparams(("parallel", "parallel"), 48),
        name="dsa_attention",
    )(qkv_a, qkv_a, qkv_a, qk_i, qk_i, w_i)


def _band_bias_kernel(rel_ref, o_ref):
    h = pl.program_id(0)
    i = lax.broadcasted_iota(jnp.int32, (BAND_Q, BAND_Q), 0)
    jj = lax.broadcasted_iota(jnp.int32, (BAND_Q, BAND_Q), 1)
    for jb in range(BAND_KB):
        j = jb * BAND_Q + jj
        idx = jnp.clip(i - j + LEFT_CHUNKS * CHUNK, -(CHUNK - 1), REL_CLIP) + (CHUNK - 1)
        cq = i >> 6
        ck = j >> 6
        val = lax.fori_loop(0, REL_SIZE, lambda r, acc: jnp.where(idx == r, rel_ref[h, r], acc),
                            jnp.zeros((BAND_Q, BAND_Q), F32))
        val = jnp.where(ck >= cq, val, NEG_INF)
        o_ref[0, jb] = jnp.where(ck <= cq + LEFT_CHUNKS, val, NEG_INF)


def _band_bias(rel_bias):
    nh = rel_bias.shape[0]
    return pl.pallas_call(
        _band_bias_kernel,
        grid=(nh,),
        in_specs=[pl.BlockSpec(memory_space=pltpu.SMEM)],
        out_specs=pl.BlockSpec((1, BAND_KB, BAND_Q, BAND_Q), lambda h: (h, 0, 0, 0)),
        out_shape=jax.ShapeDtypeStruct((nh, BAND_KB, BAND_Q, BAND_Q), F32),
        name="band_bias",
    )(rel_bias)


def _band_kernel(q_ref, k_ref, v_ref, bias_ref, o_ref):
    p = pl.program_id(1)
    first = BAND_KB - 1
    shift = jnp.maximum(first - p, 0)
    s0 = pl.multiple_of(jnp.maximum(p - first, 0) * BAND_Q, BAND_Q)
    for h in range(N_HEADS_B):
        cols = slice(h * HEAD_DIM, (h + 1) * HEAD_DIM)
        kwin = k_ref[pl.ds(s0, BAND_W), cols]
        vwin = v_ref[pl.ds(s0, BAND_W), cols]
        lg = _dot_nt(q_ref[:, cols], kwin) * SCALE
        blocks = []
        for jb in range(BAND_KB):
            src = jb + shift
            blk = lg[:, jb * BAND_Q:(jb + 1) * BAND_Q] + bias_ref[h, jnp.minimum(src, first)]
            blocks.append(jnp.where(src <= first, blk, NEG_INF))
        lg = jnp.concatenate(blocks, axis=1)
        m = jnp.max(lg, axis=1, keepdims=True)
        e = jnp.exp(lg - m)
        s = jnp.sum(e, axis=1, keepdims=True)
        o_ref[:, cols] = (_dot(e.astype(BF16), vwin) / s).astype(BF16)


def _band_attention(qkv_b, bias, batch, seq):
    nq = seq // BAND_Q
    hq = N_HEADS_B * HEAD_DIM
    qmap = lambda b, p: (b * nq + p, 0)
    return pl.pallas_call(
        _band_kernel,
        grid=(batch, nq),
        in_specs=[pl.BlockSpec((BAND_Q, hq), qmap),
                  pl.BlockSpec((seq, hq), lambda b, p: (b, 1)),
                  pl.BlockSpec((seq, hq), lambda b, p: (b, 2)),
                  pl.BlockSpec(bias.shape, lambda b, p: (0, 0, 0, 0))],
        out_specs=pl.BlockSpec((BAND_Q, hq), qmap),
        out_shape=jax.ShapeDtypeStruct((batch * seq, hq), BF16),
        compiler_params=_cparams(("parallel", "parallel"), 48),
        name="band_attention",
    )(qkv_b, qkv_b, qkv_b, bias)


def _sb_kernel(q_ref, k_ref, v_ref, o_ref, *, tq):
    i = pl.program_id(2)
    t0 = i * tq
    q = q_ref[...]
    r3 = lax.broadcasted_iota(jnp.int32, (3 * tq, tq), 0)
    c3 = lax.broadcasted_iota(jnp.int32, (3 * tq, tq), 1)
    later = jnp.where((r3 & (tq - 1)) > c3, 1.0, 0.0).astype(BF16)
    row = lax.broadcasted_iota(jnp.int32, (tq, tq), 0)
    col = lax.broadcasted_iota(jnp.int32, (tq, tq), 1)

    def block(j, carry):
        tail, acc = carry
        s0 = pl.multiple_of((i - j) * tq, tq)
        z = _dot_nt(q, k_ref[pl.ds(s0, tq), :]) * SCALE
        past = (s0 + col) < (t0 + row)
        sp = jnp.log1p(jnp.exp(-jnp.abs(z)))
        log_beta = jnp.minimum(z, 0.0) - sp
        log_keep = jnp.where(past, -jnp.maximum(z, 0.0) - sp, 0.0)
        hi = log_keep.astype(BF16)
        r1 = log_keep - hi.astype(F32)
        mid = r1.astype(BF16)
        lo = (r1 - mid.astype(F32)).astype(BF16)
        after = _dot(jnp.concatenate([hi, mid, lo], axis=1), later)
        a = jnp.where(past, jnp.exp(log_beta + after + tail), 0.0)
        acc = acc + _dot(a.astype(BF16), v_ref[pl.ds(s0, tq), :])
        return tail + jnp.sum(log_keep, axis=1, keepdims=True), acc

    _, acc = lax.fori_loop(0, i + 1, block, (jnp.zeros((tq, 1), F32), jnp.zeros((tq, HEAD_DIM), F32)))
    o_ref[...] = acc.astype(BF16)


def _stick_breaking(qkv, batch, seq, tq):
    nq = seq // tq
    nh = N_HEADS_C
    qmap = lambda b, h, i: (b * nq + i, h)
    return pl.pallas_call(
        functools.partial(_sb_kernel, tq=tq),
        grid=(batch, nh, nq),
        in_specs=[pl.BlockSpec((tq, HEAD_DIM), qmap),
                  pl.BlockSpec((seq, HEAD_DIM), lambda b, h, i: (b, nh + h)),
                  pl.BlockSpec((seq, HEAD_DIM), lambda b, h, i: (b, 2 * nh + h))],
        out_specs=pl.BlockSpec((tq, HEAD_DIM), qmap),
        out_shape=jax.ShapeDtypeStruct((batch * seq, nh * HEAD_DIM), BF16),
        compiler_params=_cparams(("parallel", "parallel", "parallel"), 32),
        name="stick_breaking",
    )(qkv, qkv, qkv)


def _xattn_kernel(xb_ref, xf_ref, kv_ref, wq_ref, wo_ref, g_ref, b_ref, of_ref, ob_ref):
    q = _dot(xb_ref[...], wq_ref[...]).astype(BF16)
    hk = N_HEADS_X * HEAD_DIM
    heads = []
    for h in range(N_HEADS_X):
        cols = slice(h * HEAD_DIM, (h + 1) * HEAD_DIM)
        lg = _dot_nt(q[:, cols], kv_ref[:, cols]) * SCALE
        m = jnp.max(lg, axis=1, keepdims=True)
        e = jnp.exp(lg - m)
        s = jnp.sum(e, axis=1, keepdims=True)
        heads.append((_dot(e.astype(BF16), kv_ref[:, hk + h * HEAD_DIM:hk + (h + 1) * HEAD_DIM]) / s).astype(BF16))
    o = jnp.concatenate(heads, axis=1)
    y = _layer_norm(ALPHA * xf_ref[...] + _dot(o, wo_ref[...]), g_ref[...], b_ref[...])
    of_ref[...] = y
    ob_ref[...] = y.astype(BF16)


def _xattn(xb, xf, kv, wq, wo, g, b, seq, tm):
    m, n = xf.shape
    per_seq = seq // tm
    row = lambda i: (i, 0)
    const = lambda i: (0, 0)
    return pl.pallas_call(
        _xattn_kernel,
        grid=(m // tm,),
        in_specs=[pl.BlockSpec((tm, n), row), pl.BlockSpec((tm, n), row),
                  pl.BlockSpec((MEM_LEN, kv.shape[1]), lambda i: (i // per_seq, 0)),
                  pl.BlockSpec(wq.shape, const), pl.BlockSpec(wo.shape, const),
                  pl.BlockSpec((1, n), const), pl.BlockSpec((1, n), const)],
        out_specs=[pl.BlockSpec((tm, n), row), pl.BlockSpec((tm, n), row)],
        out_shape=[jax.ShapeDtypeStruct((m, n), F32), jax.ShapeDtypeStruct((m, n), BF16)],
        compiler_params=_cparams(("parallel",), 56),
        name="xattn_ln",
    )(xb, xf, kv, wq, wo, g, b)


def _rope_tables(seq):
    pos = jnp.arange(seq).astype(F32)[:, None]

    def half_tables(half):
        inv_freq = ROPE_THETA ** (-jnp.arange(half, dtype=F32) / half)
        ang = pos * inv_freq[None, :]
        return jnp.cos(ang), jnp.sin(ang)

    c64, s64 = half_tables(HEAD_DIM // 2)
    c32, s32 = half_tables(IDX_DIM // 2)
    head = (jnp.concatenate([c64, c64], 1), jnp.concatenate([-s64, s64], 1))
    idx = (jnp.concatenate([c32, c32, c32, c32], 1), jnp.concatenate([-s32, s32, -s32, s32], 1))
    rest = LANE - IDX_DIM
    idx_key = (jnp.concatenate([c32, c32, jnp.ones((seq, rest), F32)], 1),
               jnp.concatenate([-s32, s32, jnp.zeros((seq, rest), F32)], 1))
    return head, idx, idx_key


def kernel(x, mem, ln_g, ln_b, ffn_in, ffn_out, xattn_q, xattn_kv, xattn_o,
           even_in, even_out, even_rel_bias, odd_in, odd_out):
    batch, seq, d = x.shape
    t = batch * seq
    xf = x.reshape(t, d)
    xb = xf.astype(BF16)
    memb = mem.reshape(batch * MEM_LEN, d).astype(BF16)
    tab_head, tab_idx, tab_idx_key = _rope_tables(seq)

    modes_a = ((0, HEAD_DIM),) * (N_HEADS_A + 1) + (None,)
    modes_i = ((0, IDX_DIM),) * ((IDX_HEADS * IDX_DIM) // LANE) + ((1, IDX_DIM),)

    def ln_params(layer, j):
        return ln_g[layer, j][None, :], ln_b[layer, j][None, :]

    def ffn(xf, xb, layer, j):
        g, b = ln_params(layer, 3 * j)
        h = _ffn_in(xb, ffn_in[layer, j].astype(BF16), 1024, 512)
        return _ffn_out(h, ffn_out[layer, j].astype(BF16), xf, g, b, 512, FFN_DIM // 4, 0.5)

    for layer in range(DEPTH):
        xf, xb = ffn(xf, xb, layer, 0)

        g, b = ln_params(layer, 1)
        if layer % 2 == 0:
            w_in = even_in[layer // 2]
            w_a = w_in[:, :A_COLS].astype(BF16)
            w_i = jnp.pad(w_in[:, A_COLS:A_COLS + I_COLS], ((0, 0), (0, I_PAD - I_COLS))).astype(BF16)
            w_b = w_in[:, A_COLS + I_COLS:].astype(BF16)
            qkv_a = _proj_rope(xb, w_a, [tab_head], modes_a, seq, 512, False, "proj_dsa")[0]
            qk_i, w_idx = _proj_rope(xb, w_i, [tab_idx, tab_idx_key], modes_i, seq, 512, True, "proj_idx")
            qkv_b = _matmul(xb, w_b, 1024, 1024, "proj_band")
            o_a = _dsa_attention(qkv_a, qk_i, w_idx, batch, seq)
            o_b = _band_attention(qkv_b, _band_bias(even_rel_bias[layer // 2]), batch, seq)
            xf, xb = _out_ln([o_a, o_b], even_out[layer // 2].astype(BF16), xf, g, b, 512, "even_out_ln")
        else:
            qkv = _matmul(xb, odd_in[layer // 2].astype(BF16), 1024, 1024, "proj_odd")
            o = _stick_breaking(qkv, batch, seq, 256)
            xf, xb = _out_ln([o], odd_out[layer // 2].astype(BF16), xf, g, b, 512, "odd_out_ln")

        g, b = ln_params(layer, 2)
        kv = _matmul(memb, xattn_kv[layer].astype(BF16), 1024, 1024, "proj_mem")
        xf, xb = _xattn(xb, xf, kv, xattn_q[layer].astype(BF16), xattn_o[layer].astype(BF16), g, b, seq, 512)

        xf, xb = ffn(xf, xb, layer, 1)

    return xf.reshape(batch, seq, d)
```

```python
import functools

import jax
import jax.numpy as jnp
from jax import lax
from jax.experimental import pallas as pl
from jax.experimental.pallas import tpu as pltpu

F32 = jnp.float32
BF16 = jnp.bfloat16

D_MODEL = 2048
DEPTH = 4
CHUNK = 64
MEM_LEN = 256
HEAD_DIM = 128
ROPE_THETA = 10000.0
LN_EPS = 1e-5
NEG_INF = -1e30
N_HEADS_A = 8
N_HEADS_B = 8
IDX_HEADS = 16
IDX_DIM = 64
TOPK_MAX = 256
LEFT_CHUNKS = 8
REL_CLIP = 128
REL_SIZE = CHUNK + REL_CLIP
N_HEADS_C = 16
N_HEADS_X = 4
FFN_DIM = ((8 * D_MODEL // 3 + 255) // 256) * 256
ALPHA = (2.0 * DEPTH) ** 0.25
SCALE = HEAD_DIM ** -0.5
LOG2_E = 1.4426950408889634

LANE = 128
INT_MIN = -2147483648
MIB = 1024 * 1024

A_COLS = (N_HEADS_A + 2) * HEAD_DIM
I_COLS = IDX_HEADS * IDX_DIM + IDX_DIM + IDX_HEADS
I_PAD = ((I_COLS + LANE - 1) // LANE) * LANE
W_IDX_LANE = IDX_DIM
B_COLS = 3 * N_HEADS_B * HEAD_DIM

BAND_Q = 2 * CHUNK
BAND_KB = (LEFT_CHUNKS * CHUNK) // BAND_Q + 1
BAND_W = BAND_KB * BAND_Q

DSA_KEY_SPAN = 512
DSA_ROW_GROUP = 32


def _cparams(sem, vmem_mib):
    return pltpu.CompilerParams(dimension_semantics=sem, vmem_limit_bytes=vmem_mib * MIB)


def _dot(a, b):
    return jnp.dot(a, b, preferred_element_type=F32)


def _dot_nt(a, b):
    return lax.dot_general(a, b, (((1,), (1,)), ((), ())), preferred_element_type=F32)


def _layer_norm(y, g, b):
    mu = jnp.mean(y, axis=-1, keepdims=True)
    d = y - mu
    var = jnp.mean(d * d, axis=-1, keepdims=True)
    return d * lax.rsqrt(var + LN_EPS) * g + b


def _mm_kernel(x_ref, w_ref, o_ref, *scratch):
    if scratch:
        wb_ref, = scratch

        @pl.when(pl.program_id(1) == 0)
        def _():
            wb_ref[...] = w_ref[...].astype(BF16)
    else:
        wb_ref = w_ref
    o_ref[...] = _dot(x_ref[...], wb_ref[...]).astype(o_ref.dtype)


def _matmul(x, w, tm, tn, name, lead=()):
    m, k = x.shape
    n = w.shape[-1]
    cast = w.dtype != BF16
    return pl.pallas_call(
        _mm_kernel,
        grid=(n // tn, m // tm),
        in_specs=[pl.BlockSpec((tm, k), lambda j, i: (i, 0)),
                  pl.BlockSpec((None,) * len(lead) + (k, tn), lambda j, i: lead + (0, j))],
        out_specs=pl.BlockSpec((tm, tn), lambda j, i: (i, j)),
        out_shape=jax.ShapeDtypeStruct((m, n), BF16),
        scratch_shapes=[pltpu.VMEM((k, tn), BF16)] if cast else [],
        compiler_params=_cparams(("parallel", "arbitrary"), 48),
        name=name,
    )(x, w)


def _rope_group(x, cos, sin, width):
    if width == LANE:
        partner = pltpu.roll(x, LANE // 2, 1)
    else:
        lane = lax.broadcasted_iota(jnp.int32, x.shape, 1)
        half = width // 2
        partner = jnp.where((lane & half) == 0, pltpu.roll(x, LANE - half, 1), pltpu.roll(x, half, 1))
    return x * cos + partner * sin


def _proj_rope_kernel(x_ref, w_ref, *rest, modes, n_tab, f32_tail):
    tabs = rest[:2 * n_tab]
    outs = rest[2 * n_tab:]
    acc = _dot(x_ref[...], w_ref[...])
    for g, mode in enumerate(modes):
        blk = acc[:, g * LANE:(g + 1) * LANE]
        if mode is not None:
            tid, width = mode
            blk = _rope_group(blk, tabs[2 * tid][...], tabs[2 * tid + 1][...], width)
        outs[0][:, g * LANE:(g + 1) * LANE] = blk.astype(BF16)
    if f32_tail:
        outs[1][...] = acc[:, -LANE:]


def _proj_rope(x, w, tabs, modes, seq, tm, f32_tail, name):
    m, k = x.shape
    n = w.shape[1]
    per_seq = seq // tm
    in_specs = [pl.BlockSpec((tm, k), lambda i: (i, 0)), pl.BlockSpec((k, n), lambda i: (0, 0))]
    args = [x, w]
    for cos, sin in tabs:
        in_specs += [pl.BlockSpec((tm, LANE), lambda i: (i % per_seq, 0))] * 2
        args += [cos, sin]
    out_specs = [pl.BlockSpec((tm, n), lambda i: (i, 0))]
    out_shape = [jax.ShapeDtypeStruct((m, n), BF16)]
    if f32_tail:
        out_specs.append(pl.BlockSpec((tm, LANE), lambda i: (i, 0)))
        out_shape.append(jax.ShapeDtypeStruct((m, LANE), F32))
    return pl.pallas_call(
        functools.partial(_proj_rope_kernel, modes=modes, n_tab=len(tabs), f32_tail=f32_tail),
        grid=(m // tm,),
        in_specs=in_specs,
        out_specs=out_specs,
        out_shape=out_shape,
        compiler_params=_cparams(("parallel",), 48),
        name=name,
    )(*args)


def _ffn_in_kernel(x_ref, wa_ref, wg_ref, o_ref, wab_ref, wgb_ref):
    @pl.when(pl.program_id(1) == 0)
    def _():
        wab_ref[...] = wa_ref[...].astype(BF16)
        wgb_ref[...] = wg_ref[...].astype(BF16)

    x = x_ref[...]
    a = _dot(x, wab_ref[...])
    g = _dot(x, wgb_ref[...])
    o_ref[...] = (a * jax.nn.sigmoid(a) * g).astype(o_ref.dtype)


def _ffn_in(xb, w, lead, tm, tf):
    m, k = xb.shape
    nf = FFN_DIM // tf
    wblock = (None,) * len(lead) + (k, tf)
    return pl.pallas_call(
        _ffn_in_kernel,
        grid=(nf, m // tm),
        in_specs=[pl.BlockSpec((tm, k), lambda j, i: (i, 0)),
                  pl.BlockSpec(wblock, lambda j, i: lead + (0, j)),
                  pl.BlockSpec(wblock, lambda j, i: lead + (0, j + nf))],
        out_specs=pl.BlockSpec((tm, tf), lambda j, i: (i, j)),
        out_shape=jax.ShapeDtypeStruct((m, FFN_DIM), BF16),
        scratch_shapes=[pltpu.VMEM((k, tf), BF16), pltpu.VMEM((k, tf), BF16)],
        compiler_params=_cparams(("parallel", "arbitrary"), 56),
        name="ffn_in",
    )(xb, w, w)


def _ffn_out_kernel(h_ref, w_ref, x_ref, g_ref, b_ref, of_ref, ob_ref, acc_ref, *, nk, scale):
    k = pl.program_id(1)

    @pl.when(k == 0)
    def _():
        acc_ref[...] = _dot(h_ref[...], w_ref[...])

    @pl.when(k > 0)
    def _():
        acc_ref[...] += _dot(h_ref[...], w_ref[...])

    @pl.when(k == nk - 1)
    def _():
        y = _layer_norm(ALPHA * x_ref[...] + scale * acc_ref[...], g_ref[...], b_ref[...])
        of_ref[...] = y
        ob_ref[...] = y.astype(BF16)


def _ffn_out(h, w, xf, g, b, tm, tk, scale):
    m, kdim = h.shape
    n = w.shape[1]
    nk = kdim // tk
    row = lambda i, k: (i, 0)
    return pl.pallas_call(
        functools.partial(_ffn_out_kernel, nk=nk, scale=scale),
        grid=(m // tm, nk),
        in_specs=[pl.BlockSpec((tm, tk), lambda i, k: (i, k)),
                  pl.BlockSpec((tk, n), lambda i, k: (k, 0)),
                  pl.BlockSpec((tm, n), row),
                  pl.BlockSpec((1, n), lambda i, k: (0, 0)),
                  pl.BlockSpec((1, n), lambda i, k: (0, 0))],
        out_specs=[pl.BlockSpec((tm, n), row), pl.BlockSpec((tm, n), row)],
        out_shape=[jax.ShapeDtypeStruct((m, n), F32), jax.ShapeDtypeStruct((m, n), BF16)],
        scratch_shapes=[pltpu.VMEM((tm, n), F32)],
        compiler_params=_cparams(("parallel", "arbitrary"), 56),
        name="ffn_out_ln",
    )(h, w, xf, g, b)


def _out_ln_kernel(*refs, n_lhs):
    lhs = refs[:n_lhs]
    ws = refs[n_lhs:2 * n_lhs]
    x_ref, g_ref, b_ref, of_ref, ob_ref = refs[2 * n_lhs:]
    acc = _dot(lhs[0][...], ws[0][...])
    for l_ref, w_ref in zip(lhs[1:], ws[1:]):
        acc = acc + _dot(l_ref[...], w_ref[...])
    y = _layer_norm(ALPHA * x_ref[...] + acc, g_ref[...], b_ref[...])
    of_ref[...] = y
    ob_ref[...] = y.astype(BF16)


def _out_ln(lhs_list, w, xf, g, b, tm, name):
    m, n = xf.shape
    row = lambda i: (i, 0)
    in_specs, args, off = [], [], 0
    for l in lhs_list:
        in_specs.append(pl.BlockSpec((tm, l.shape[1]), row))
        args.append(l)
    for l in lhs_list:
        kl = l.shape[1]
        in_specs.append(pl.BlockSpec((kl, n), functools.partial(lambda i, o: (o, 0), o=off // kl)))
        args.append(w)
        off += kl
    in_specs += [pl.BlockSpec((tm, n), row), pl.BlockSpec((1, n), lambda i: (0, 0)), pl.BlockSpec((1, n), lambda i: (0, 0))]
    args += [xf, g, b]
    return pl.pallas_call(
        functools.partial(_out_ln_kernel, n_lhs=len(lhs_list)),
        grid=(m // tm,),
        in_specs=in_specs,
        out_specs=[pl.BlockSpec((tm, n), row), pl.BlockSpec((tm, n), row)],
        out_shape=[jax.ShapeDtypeStruct((m, n), F32), jax.ShapeDtypeStruct((m, n), BF16)],
        compiler_params=_cparams(("parallel",), 56),
        name=name,
    )(*args)


def _dsa_kernel(q_ref, k_ref, v_ref, qi_ref, ki_ref, w_ref, o_ref, key_ref, *, tq, seq, topk):
    i = pl.program_id(1)
    per_span = DSA_KEY_SPAN // tq
    for n in range(1, seq // DSA_KEY_SPAN + 1):
        pl.when(i // per_span == n - 1)(functools.partial(
            _dsa_body, i, q_ref, k_ref, v_ref, qi_ref, ki_ref, w_ref, o_ref, key_ref,
            tq=tq, nk=n * DSA_KEY_SPAN, topk=topk))


def _dsa_body(i, q_ref, k_ref, v_ref, qi_ref, ki_ref, w_ref, o_ref, key_ref, *, tq, nk, topk):
    ki = ki_ref[:nk, :IDX_DIM]
    w = w_ref[...]
    score = jnp.zeros((tq, nk), F32)
    for h in range(IDX_HEADS):
        rel = jnp.maximum(_dot_nt(qi_ref[:, h * IDX_DIM:(h + 1) * IDX_DIM], ki), 0.0)
        score = score + rel * w[:, W_IDX_LANE + h:W_IDX_LANE + h + 1]
    row = lax.broadcasted_iota(jnp.int32, (tq, nk), 0)
    col = lax.broadcasted_iota(jnp.int32, (tq, nk), 1)
    allowed = (col >> 6) <= ((i * tq + row) >> 6)
    score = jnp.where(allowed, score, NEG_INF)
    bits = lax.bitcast_convert_type(score, jnp.int32)
    key = bits ^ ((bits >> 31) & 0x7FFFFFFF)
    key_ref[:, :nk] = key

    groups = [slice(r * DSA_ROW_GROUP, (r + 1) * DSA_ROW_GROUP) for r in range(tq // DSA_ROW_GROUP)]

    def select_bit(j, ts):
        bit = lax.shift_left(jnp.int32(1), 31 - j)
        out = []
        for rows, t in zip(groups, ts):
            cand = t | bit
            cnt = jnp.sum(jnp.where(key_ref[rows, :nk] >= (cand ^ INT_MIN), 1.0, 0.0), axis=1, keepdims=True)
            out.append(jnp.where(cnt >= topk, cand, t))
        return tuple(out)

    ts = lax.fori_loop(0, 32, select_bit, tuple(jnp.zeros((DSA_ROW_GROUP, 1), jnp.int32) for _ in groups))
    thr = jnp.maximum(jnp.concatenate(ts, axis=0) ^ INT_MIN, INT_MIN + 1)
    key_ref[:, :nk] = jnp.where(allowed, key_ref[:, :nk], INT_MIN)

    k = k_ref[:nk, :]
    v = v_ref[:nk, :]
    for h in range(N_HEADS_A):
        cols = slice(h * HEAD_DIM, (h + 1) * HEAD_DIM)
        lg = _dot_nt(q_ref[:, cols], k) * SCALE
        lg = jnp.where(key_ref[:, :nk] >= thr, lg, NEG_INF)
        m = jnp.max(lg, axis=1, keepdims=True)
        e = jnp.exp(lg - m)
        s = jnp.sum(e, axis=1, keepdims=True)
        o_ref[:, cols] = (_dot(e.astype(BF16), v) / s).astype(BF16)


def _dsa_attention(qkv_a, qk_i, w_i, batch, seq):
    tq = 2 * CHUNK
    nq = seq // tq
    hq = N_HEADS_A * HEAD_DIM
    kcol = hq // LANE
    icol = (IDX_HEADS * IDX_DIM) // LANE
    qmap = lambda b, i: (b * nq + i, 0)
    return pl.pallas_call(
        functools.partial(_dsa_kernel, tq=tq, seq=seq, topk=min(TOPK_MAX, seq // 4)),
        grid=(batch, nq),
        in_specs=[pl.BlockSpec((tq, hq), qmap),
                  pl.BlockSpec((seq, LANE), lambda b, i: (b, kcol)),
                  pl.BlockSpec((seq, LANE), lambda b, i: (b, kcol + 1)),
                  pl.BlockSpec((tq, IDX_HEADS * IDX_DIM), qmap),
                  pl.BlockSpec((seq, LANE), lambda b, i: (b, icol)),
                  pl.BlockSpec((tq, LANE), qmap)],
        out_specs=pl.BlockSpec((tq, hq), qmap),
        out_shape=jax.ShapeDtypeStruct((batch * seq, hq), BF16),
        scratch_shapes=[pltpu.VMEM((tq, seq), jnp.int32)],
        compiler_params=_cparams(("parallel", "parallel"), 48),
        name="dsa_attention",
    )(qkv_a, qkv_a, qkv_a, qk_i, qk_i, w_i)


def _band_bias_kernel(rel_ref, o_ref):
    h = pl.program_id(0)
    i = lax.broadcasted_iota(jnp.int32, (BAND_Q, BAND_Q), 0)
    jj = lax.broadcasted_iota(jnp.int32, (BAND_Q, BAND_Q), 1)
    for jb in range(BAND_KB):
        j = jb * BAND_Q + jj
        idx = jnp.clip(i - j + LEFT_CHUNKS * CHUNK, -(CHUNK - 1), REL_CLIP) + (CHUNK - 1)
        cq = i >> 6
        ck = j >> 6
        d_lo = LEFT_CHUNKS * CHUNK - (jb + 1) * BAND_Q + 1
        d_hi = LEFT_CHUNKS * CHUNK - jb * BAND_Q + BAND_Q - 1
        r_lo = min(max(d_lo, -(CHUNK - 1)), REL_CLIP) + (CHUNK - 1)
        r_hi = min(max(d_hi, -(CHUNK - 1)), REL_CLIP) + (CHUNK - 1)
        val = lax.fori_loop(r_lo, r_hi + 1, lambda r, acc: jnp.where(idx == r, rel_ref[h, r], acc),
                            jnp.zeros((BAND_Q, BAND_Q), F32))
        val = jnp.where(ck >= cq, val, NEG_INF)
        o_ref[0, jb] = jnp.where(ck <= cq + LEFT_CHUNKS, val, NEG_INF)


def _band_bias(rel_bias):
    nh = rel_bias.shape[0]
    return pl.pallas_call(
        _band_bias_kernel,
        grid=(nh,),
        in_specs=[pl.BlockSpec(memory_space=pltpu.SMEM)],
        out_specs=pl.BlockSpec((1, BAND_KB, BAND_Q, BAND_Q), lambda h: (h, 0, 0, 0)),
        out_shape=jax.ShapeDtypeStruct((nh, BAND_KB, BAND_Q, BAND_Q), F32),
        name="band_bias",
    )(rel_bias)


def _band_kernel(q_ref, k_ref, v_ref, bias_ref, o_ref):
    p = pl.program_id(1)
    first = BAND_KB - 1
    shift = jnp.maximum(first - p, 0)
    s0 = pl.multiple_of(jnp.maximum(p - first, 0) * BAND_Q, BAND_Q)
    for h in range(N_HEADS_B):
        cols = slice(h * HEAD_DIM, (h + 1) * HEAD_DIM)
        kwin = k_ref[pl.ds(s0, BAND_W), cols]
        vwin = v_ref[pl.ds(s0, BAND_W), cols]
        lg = _dot_nt(q_ref[:, cols], kwin) * SCALE
        blocks = []
        for jb in range(BAND_KB):
            src = jb + shift
            blk = lg[:, jb * BAND_Q:(jb + 1) * BAND_Q] + bias_ref[h, jnp.minimum(src, first)]
            blocks.append(jnp.where(src <= first, blk, NEG_INF))
        lg = jnp.concatenate(blocks, axis=1)
        m = jnp.max(lg, axis=1, keepdims=True)
        e = jnp.exp(lg - m)
        s = jnp.sum(e, axis=1, keepdims=True)
        o_ref[:, cols] = (_dot(e.astype(BF16), vwin) / s).astype(BF16)


def _band_attention(qkv_b, bias, batch, seq):
    nq = seq // BAND_Q
    hq = N_HEADS_B * HEAD_DIM
    qmap = lambda b, p: (b * nq + p, 0)
    return pl.pallas_call(
        _band_kernel,
        grid=(batch, nq),
        in_specs=[pl.BlockSpec((BAND_Q, hq), qmap),
                  pl.BlockSpec((seq, hq), lambda b, p: (b, 1)),
                  pl.BlockSpec((seq, hq), lambda b, p: (b, 2)),
                  pl.BlockSpec(bias.shape, lambda b, p: (0, 0, 0, 0))],
        out_specs=pl.BlockSpec((BAND_Q, hq), qmap),
        out_shape=jax.ShapeDtypeStruct((batch * seq, hq), BF16),
        compiler_params=_cparams(("parallel", "parallel"), 48),
        name="band_attention",
    )(qkv_b, qkv_b, qkv_b, bias)


def _sb_blocks(qs, k_blks, v_blks, later, carry, past):
    zs = [_dot_nt(q, k) * (SCALE * LOG2_E) for q, k in zip(qs, k_blks)]
    log_betas, drops, splits = [], [], []
    for z2 in zs:
        neg_abs = lax.bitcast_convert_type(lax.bitcast_convert_type(z2, jnp.int32) | INT_MIN, F32)
        drop = jnp.maximum(z2, 0.0) + jnp.log2(1.0 + jnp.exp2(neg_abs))
        log_betas.append(z2 - drop)
        if past is not None:
            drop = jnp.where(past, drop, 0.0)
        drops.append(drop)
        hi = drop.astype(BF16)
        r1 = drop - hi.astype(F32)
        mid = r1.astype(BF16)
        lo = (r1 - mid.astype(F32)).astype(BF16)
        splits.append(jnp.concatenate([hi, mid, lo], axis=1))
    afters = [_dot(s, later) for s in splits]
    probs, tails = [], []
    for h, (log_beta, drop, after) in enumerate(zip(log_betas, drops, afters)):
        row_sum = jnp.sum(drop, axis=1, keepdims=True)
        if carry is None:
            a = jnp.exp2(log_beta - after)
            tails.append(row_sum)
        else:
            a = jnp.exp2(log_beta - carry[2 * h] - after)
            tails.append(carry[2 * h] + row_sum)
        if past is not None:
            a = jnp.where(past, a, 0.0)
        probs.append(a.astype(BF16))
    pvs = [_dot(a, v) for a, v in zip(probs, v_blks)]
    out = []
    for h, (tail, pv) in enumerate(zip(tails, pvs)):
        out += [tail, pv if carry is None else carry[2 * h + 1] + pv]
    return tuple(out)


def _sb_kernel(q_ref, k_ref, v_ref, o_ref, *, tq, nh):
    i = pl.program_id(2)
    r3 = lax.broadcasted_iota(jnp.int32, (3 * tq, tq), 0)
    c3 = lax.broadcasted_iota(jnp.int32, (3 * tq, tq), 1)
    later = jnp.where((r3 & (tq - 1)) > c3, 1.0, 0.0).astype(BF16)
    row = lax.broadcasted_iota(jnp.int32, (tq, tq), 0)
    col = lax.broadcasted_iota(jnp.int32, (tq, tq), 1)
    heads = [slice(h * HEAD_DIM, (h + 1) * HEAD_DIM) for h in range(nh)]

    def walk(s0, carry, past):
        return _sb_blocks([q_ref[:, cols] for cols in heads],
                          [k_ref[pl.ds(s0, tq), cols] for cols in heads],
                          [v_ref[pl.ds(s0, tq), cols] for cols in heads], later, carry, past)

    carry = walk(pl.multiple_of(i * tq, tq), None, col < row)
    carry = lax.fori_loop(0, i, lambda j, c: walk(pl.multiple_of((i - 1 - j) * tq, tq), c, None), carry)
    for h, cols in enumerate(heads):
        o_ref[:, cols] = carry[2 * h + 1].astype(BF16)


def _stick_breaking(qkv, batch, seq, tq, nh):
    nq = seq // tq
    ng = N_HEADS_C // nh
    w = nh * HEAD_DIM
    qmap = lambda b, g, i: (b * nq + i, g)
    return pl.pallas_call(
        functools.partial(_sb_kernel, tq=tq, nh=nh),
        grid=(batch, ng, nq),
        in_specs=[pl.BlockSpec((tq, w), qmap),
                  pl.BlockSpec((seq, w), lambda b, g, i: (b, ng + g)),
                  pl.BlockSpec((seq, w), lambda b, g, i: (b, 2 * ng + g))],
        out_specs=pl.BlockSpec((tq, w), qmap),
        out_shape=jax.ShapeDtypeStruct((batch * seq, N_HEADS_C * HEAD_DIM), BF16),
        compiler_params=_cparams(("parallel", "parallel", "parallel"), 32),
        name="stick_breaking",
    )(qkv, qkv, qkv)


def _xattn_kernel(xb_ref, xf_ref, kv_ref, wq_ref, wo_ref, g_ref, b_ref, of_ref, ob_ref):
    q = _dot(xb_ref[...], wq_ref[...]).astype(BF16)
    hk = N_HEADS_X * HEAD_DIM
    heads = []
    for h in range(N_HEADS_X):
        cols = slice(h * HEAD_DIM, (h + 1) * HEAD_DIM)
        lg = _dot_nt(q[:, cols], kv_ref[:, cols]) * SCALE
        m = jnp.max(lg, axis=1, keepdims=True)
        e = jnp.exp(lg - m)
        s = jnp.sum(e, axis=1, keepdims=True)
        heads.append((_dot(e.astype(BF16), kv_ref[:, hk + h * HEAD_DIM:hk + (h + 1) * HEAD_DIM]) / s).astype(BF16))
    o = jnp.concatenate(heads, axis=1)
    y = _layer_norm(ALPHA * xf_ref[...] + _dot(o, wo_ref[...]), g_ref[...], b_ref[...])
    of_ref[...] = y
    ob_ref[...] = y.astype(BF16)


def _xattn(xb, xf, kv, wq, wo, g, b, seq, tm):
    m, n = xf.shape
    per_seq = seq // tm
    row = lambda i: (i, 0)
    const = lambda i: (0, 0)
    return pl.pallas_call(
        _xattn_kernel,
        grid=(m // tm,),
        in_specs=[pl.BlockSpec((tm, n), row), pl.BlockSpec((tm, n), row),
                  pl.BlockSpec((MEM_LEN, kv.shape[1]), lambda i: (i // per_seq, 0)),
                  pl.BlockSpec(wq.shape, const), pl.BlockSpec(wo.shape, const),
                  pl.BlockSpec((1, n), const), pl.BlockSpec((1, n), const)],
        out_specs=[pl.BlockSpec((tm, n), row), pl.BlockSpec((tm, n), row)],
        out_shape=[jax.ShapeDtypeStruct((m, n), F32), jax.ShapeDtypeStruct((m, n), BF16)],
        compiler_params=_cparams(("parallel",), 56),
        name="xattn_ln",
    )(xb, xf, kv, wq, wo, g, b)


def _rope_tables(seq):
    pos = jnp.arange(seq).astype(F32)[:, None]

    def half_tables(half):
        inv_freq = ROPE_THETA ** (-jnp.arange(half, dtype=F32) / half)
        ang = pos * inv_freq[None, :]
        return jnp.cos(ang), jnp.sin(ang)

    c64, s64 = half_tables(HEAD_DIM // 2)
    c32, s32 = half_tables(IDX_DIM // 2)
    head = (jnp.concatenate([c64, c64], 1), jnp.concatenate([-s64, s64], 1))
    idx = (jnp.concatenate([c32, c32, c32, c32], 1), jnp.concatenate([-s32, s32, -s32, s32], 1))
    rest = LANE - IDX_DIM
    idx_key = (jnp.concatenate([c32, c32, jnp.ones((seq, rest), F32)], 1),
               jnp.concatenate([-s32, s32, jnp.zeros((seq, rest), F32)], 1))
    return head, idx, idx_key


def kernel(x, mem, ln_g, ln_b, ffn_in, ffn_out, xattn_q, xattn_kv, xattn_o,
           even_in, even_out, even_rel_bias, odd_in, odd_out):
    batch, seq, d = x.shape
    t = batch * seq
    xf = x.reshape(t, d)
    xb = xf.astype(BF16)
    memb = mem.reshape(batch * MEM_LEN, d).astype(BF16)
    tab_head, tab_idx, tab_idx_key = _rope_tables(seq)

    modes_a = ((0, HEAD_DIM),) * (N_HEADS_A + 1) + (None,)
    modes_i = ((0, IDX_DIM),) * ((IDX_HEADS * IDX_DIM) // LANE) + ((1, IDX_DIM),)

    def ln_params(layer, j):
        return ln_g[layer, j][None, :], ln_b[layer, j][None, :]

    def ffn(xf, xb, layer, j):
        g, b = ln_params(layer, 3 * j)
        h = _ffn_in(xb, ffn_in, (layer, j), 1024, 512)
        return _ffn_out(h, ffn_out[layer, j].astype(BF16), xf, g, b, 512, FFN_DIM // 4, 0.5)

    for layer in range(DEPTH):
        xf, xb = ffn(xf, xb, layer, 0)

        g, b = ln_params(layer, 1)
        if layer % 2 == 0:
            w_in = even_in[layer // 2]
            w_a = w_in[:, :A_COLS].astype(BF16)
            w_i = jnp.pad(w_in[:, A_COLS:A_COLS + I_COLS], ((0, 0), (0, I_PAD - I_COLS))).astype(BF16)
            w_b = w_in[:, A_COLS + I_COLS:].astype(BF16)
            qkv_a = _proj_rope(xb, w_a, [tab_head], modes_a, seq, 512, False, "proj_dsa")[0]
            qk_i, w_idx = _proj_rope(xb, w_i, [tab_idx, tab_idx_key], modes_i, seq, 512, True, "proj_idx")
            qkv_b = _matmul(xb, w_b, 1024, 1024, "proj_band")
            o_a = _dsa_attention(qkv_a, qk_i, w_idx, batch, seq)
            o_b = _band_attention(qkv_b, _band_bias(even_rel_bias[layer // 2]), batch, seq)
            xf, xb = _out_ln([o_a, o_b], even_out[layer // 2].astype(BF16), xf, g, b, 512, "even_out_ln")
        else:
            qkv = _matmul(xb, odd_in, 1024, 1024, "proj_odd", lead=(layer // 2,))
            o = _stick_breaking(qkv, batch, seq, 256, 4)
            xf, xb = _out_ln([o], odd_out[layer // 2].astype(BF16), xf, g, b, 512, "odd_out_ln")

        g, b = ln_params(layer, 2)
        kv = _matmul(memb, xattn_kv[layer].astype(BF16), 1024, 1024, "proj_mem")
        xf, xb = _xattn(xb, xf, kv, xattn_q[layer].astype(BF16), xattn_o[layer].astype(BF16), g, b, seq, 512)

        xf, xb = ffn(xf, xb, layer, 1)

    return xf.reshape(batch, seq, d)
```

```python
import functools

import jax
import jax.numpy as jnp
from jax import lax
from jax.experimental import pallas as pl
from jax.experimental.pallas import tpu as pltpu

F32 = jnp.float32
BF16 = jnp.bfloat16

D_MODEL = 2048
DEPTH = 4
CHUNK = 64
MEM_LEN = 256
HEAD_DIM = 128
ROPE_THETA = 10000.0
LN_EPS = 1e-5
NEG_INF = -1e30
N_HEADS_A = 8
N_HEADS_B = 8
IDX_HEADS = 16
IDX_DIM = 64
TOPK_MAX = 256
LEFT_CHUNKS = 8
REL_CLIP = 128
REL_SIZE = CHUNK + REL_CLIP
N_HEADS_C = 16
N_HEADS_X = 4
FFN_DIM = ((8 * D_MODEL // 3 + 255) // 256) * 256
ALPHA = (2.0 * DEPTH) ** 0.25
SCALE = HEAD_DIM ** -0.5
LOG2_E = 1.4426950408889634

LANE = 128
INT_MIN = -2147483648
MIB = 1024 * 1024

A_COLS = (N_HEADS_A + 2) * HEAD_DIM
I_COLS = IDX_HEADS * IDX_DIM + IDX_DIM + IDX_HEADS
I_PAD = ((I_COLS + LANE - 1) // LANE) * LANE
W_IDX_LANE = IDX_DIM
B_COLS = 3 * N_HEADS_B * HEAD_DIM

BAND_Q = 2 * CHUNK
BAND_KB = (LEFT_CHUNKS * CHUNK) // BAND_Q + 1
BAND_W = BAND_KB * BAND_Q

SB_DEAD_BITS = 152.0

DSA_KEY_SPAN = 512
DSA_HEAD_GROUP = 4
DSA_ROW_GROUP = 32


def _cparams(sem, vmem_mib):
    return pltpu.CompilerParams(dimension_semantics=sem, vmem_limit_bytes=vmem_mib * MIB)


def _dot(a, b):
    return jnp.dot(a, b, preferred_element_type=F32)


def _dot_nt(a, b):
    return lax.dot_general(a, b, (((1,), (1,)), ((), ())), preferred_element_type=F32)


def _layer_norm(y, g, b):
    mu = jnp.mean(y, axis=-1, keepdims=True)
    d = y - mu
    var = jnp.mean(d * d, axis=-1, keepdims=True)
    return d * lax.rsqrt(var + LN_EPS) * g + b


def _mm_kernel(x_ref, w_ref, o_ref, *scratch):
    if scratch:
        wb_ref, = scratch

        @pl.when(pl.program_id(1) == 0)
        def _():
            wb_ref[...] = w_ref[...].astype(BF16)
    else:
        wb_ref = w_ref
    o_ref[...] = _dot(x_ref[...], wb_ref[...]).astype(o_ref.dtype)


def _matmul(x, w, tm, tn, name, lead=()):
    m, k = x.shape
    n = w.shape[-1]
    cast = w.dtype != BF16
    return pl.pallas_call(
        _mm_kernel,
        grid=(n // tn, m // tm),
        in_specs=[pl.BlockSpec((tm, k), lambda j, i: (i, 0)),
                  pl.BlockSpec((None,) * len(lead) + (k, tn), lambda j, i: lead + (0, j))],
        out_specs=pl.BlockSpec((tm, tn), lambda j, i: (i, j)),
        out_shape=jax.ShapeDtypeStruct((m, n), BF16),
        scratch_shapes=[pltpu.VMEM((k, tn), BF16)] if cast else [],
        compiler_params=_cparams(("parallel", "arbitrary"), 48),
        name=name,
    )(x, w)


def _rope_group(x, cos, sin, width):
    if width == LANE:
        partner = pltpu.roll(x, LANE // 2, 1)
    else:
        lane = lax.broadcasted_iota(jnp.int32, x.shape, 1)
        half = width // 2
        partner = jnp.where((lane & half) == 0, pltpu.roll(x, LANE - half, 1), pltpu.roll(x, half, 1))
    return x * cos + partner * sin


def _proj_rope_kernel(x_ref, w_ref, *rest, modes, n_tab, f32_tail):
    tabs = rest[:2 * n_tab]
    outs = rest[2 * n_tab:]
    acc = _dot(x_ref[...], w_ref[...])
    for g, mode in enumerate(modes):
        blk = acc[:, g * LANE:(g + 1) * LANE]
        if mode is not None:
            tid, width = mode
            blk = _rope_group(blk, tabs[2 * tid][...], tabs[2 * tid + 1][...], width)
        outs[0][:, g * LANE:(g + 1) * LANE] = blk.astype(BF16)
    if f32_tail:
        outs[1][...] = acc[:, -LANE:]


def _proj_rope(x, w, tabs, modes, seq, tm, f32_tail, name):
    m, k = x.shape
    n = w.shape[1]
    per_seq = seq // tm
    in_specs = [pl.BlockSpec((tm, k), lambda i: (i, 0)), pl.BlockSpec((k, n), lambda i: (0, 0))]
    args = [x, w]
    for cos, sin in tabs:
        in_specs += [pl.BlockSpec((tm, LANE), lambda i: (i % per_seq, 0))] * 2
        args += [cos, sin]
    out_specs = [pl.BlockSpec((tm, n), lambda i: (i, 0))]
    out_shape = [jax.ShapeDtypeStruct((m, n), BF16)]
    if f32_tail:
        out_specs.append(pl.BlockSpec((tm, LANE), lambda i: (i, 0)))
        out_shape.append(jax.ShapeDtypeStruct((m, LANE), F32))
    return pl.pallas_call(
        functools.partial(_proj_rope_kernel, modes=modes, n_tab=len(tabs), f32_tail=f32_tail),
        grid=(m // tm,),
        in_specs=in_specs,
        out_specs=out_specs,
        out_shape=out_shape,
        compiler_params=_cparams(("parallel",), 48),
        name=name,
    )(*args)


def _ffn_in_kernel(x_ref, wa_ref, wg_ref, o_ref, wab_ref, wgb_ref):
    @pl.when(pl.program_id(1) == 0)
    def _():
        wab_ref[...] = wa_ref[...].astype(BF16)
        wgb_ref[...] = wg_ref[...].astype(BF16)

    x = x_ref[...]
    a = _dot(x, wab_ref[...])
    g = _dot(x, wgb_ref[...])
    o_ref[...] = (a * jax.nn.sigmoid(a) * g).astype(o_ref.dtype)


def _ffn_in(xb, w, lead, tm, tf):
    m, k = xb.shape
    nf = FFN_DIM // tf
    wblock = (None,) * len(lead) + (k, tf)
    return pl.pallas_call(
        _ffn_in_kernel,
        grid=(nf, m // tm),
        in_specs=[pl.BlockSpec((tm, k), lambda j, i: (i, 0)),
                  pl.BlockSpec(wblock, lambda j, i: lead + (0, j)),
                  pl.BlockSpec(wblock, lambda j, i: lead + (0, j + nf))],
        out_specs=pl.BlockSpec((tm, tf), lambda j, i: (i, j)),
        out_shape=jax.ShapeDtypeStruct((m, FFN_DIM), BF16),
        scratch_shapes=[pltpu.VMEM((k, tf), BF16), pltpu.VMEM((k, tf), BF16)],
        compiler_params=_cparams(("parallel", "arbitrary"), 56),
        name="ffn_in",
    )(xb, w, w)


def _ffn_out_kernel(h_ref, w_ref, x_ref, g_ref, b_ref, of_ref, ob_ref, *, nk, scale):
    k = pl.program_id(1)

    @pl.when(k == 0)
    def _():
        of_ref[...] = _dot(h_ref[...], w_ref[...])

    @pl.when(k > 0)
    def _():
        of_ref[...] += _dot(h_ref[...], w_ref[...])

    @pl.when(k == nk - 1)
    def _():
        y = _layer_norm(ALPHA * x_ref[...] + scale * of_ref[...], g_ref[...], b_ref[...])
        of_ref[...] = y
        ob_ref[...] = y.astype(BF16)


def _ffn_out(h, w, lead, xf, g, b, tm, tk, scale):
    m, kdim = h.shape
    n = w.shape[-1]
    nk = kdim // tk
    row = lambda i, k: (i, 0)
    return pl.pallas_call(
        functools.partial(_ffn_out_kernel, nk=nk, scale=scale),
        grid=(m // tm, nk),
        in_specs=[pl.BlockSpec((tm, tk), lambda i, k: (i, k)),
                  pl.BlockSpec((None,) * len(lead) + (tk, n), lambda i, k: lead + (k, 0)),
                  pl.BlockSpec((tm, n), row),
                  pl.BlockSpec((1, n), lambda i, k: (0, 0)),
                  pl.BlockSpec((1, n), lambda i, k: (0, 0))],
        out_specs=[pl.BlockSpec((tm, n), row), pl.BlockSpec((tm, n), row)],
        out_shape=[jax.ShapeDtypeStruct((m, n), F32), jax.ShapeDtypeStruct((m, n), BF16)],
        compiler_params=_cparams(("parallel", "arbitrary"), 60),
        name="ffn_out_ln",
    )(h, w, xf, g, b)


def _out_ln_kernel(*refs, n_lhs):
    lhs = refs[:n_lhs]
    ws = refs[n_lhs:2 * n_lhs]
    x_ref, g_ref, b_ref, of_ref, ob_ref = refs[2 * n_lhs:]
    acc = _dot(lhs[0][...], ws[0][...])
    for l_ref, w_ref in zip(lhs[1:], ws[1:]):
        acc = acc + _dot(l_ref[...], w_ref[...])
    y = _layer_norm(ALPHA * x_ref[...] + acc, g_ref[...], b_ref[...])
    of_ref[...] = y
    ob_ref[...] = y.astype(BF16)


def _out_ln(lhs_list, w, xf, g, b, tm, name):
    m, n = xf.shape
    row = lambda i: (i, 0)
    in_specs, args, off = [], [], 0
    for l in lhs_list:
        in_specs.append(pl.BlockSpec((tm, l.shape[1]), row))
        args.append(l)
    for l in lhs_list:
        kl = l.shape[1]
        in_specs.append(pl.BlockSpec((kl, n), functools.partial(lambda i, o: (o, 0), o=off // kl)))
        args.append(w)
        off += kl
    in_specs += [pl.BlockSpec((tm, n), row), pl.BlockSpec((1, n), lambda i: (0, 0)), pl.BlockSpec((1, n), lambda i: (0, 0))]
    args += [xf, g, b]
    return pl.pallas_call(
        functools.partial(_out_ln_kernel, n_lhs=len(lhs_list)),
        grid=(m // tm,),
        in_specs=in_specs,
        out_specs=[pl.BlockSpec((tm, n), row), pl.BlockSpec((tm, n), row)],
        out_shape=[jax.ShapeDtypeStruct((m, n), F32), jax.ShapeDtypeStruct((m, n), BF16)],
        compiler_params=_cparams(("parallel",), 56),
        name=name,
    )(*args)


def _dsa_kernel(q_ref, k_ref, v_ref, qi_ref, ki_ref, w_ref, o_ref, key_ref, *, tq, seq, topk):
    i = pl.program_id(1)
    per_span = DSA_KEY_SPAN // tq
    for n in range(1, seq // DSA_KEY_SPAN + 1):
        pl.when(i // per_span == n - 1)(functools.partial(
            _dsa_body, i, q_ref, k_ref, v_ref, qi_ref, ki_ref, w_ref, o_ref, key_ref,
            tq=tq, nk=n * DSA_KEY_SPAN, topk=topk))


def _dsa_body(i, q_ref, k_ref, v_ref, qi_ref, ki_ref, w_ref, o_ref, key_ref, *, tq, nk, topk):
    w = w_ref[...]
    span = DSA_KEY_SPAN
    last = slice(nk - span, nk)
    row = lax.broadcasted_iota(jnp.int32, (tq, span), 0)
    col = lax.broadcasted_iota(jnp.int32, (tq, span), 1)
    allowed = ((nk - span + col) >> 6) <= ((i * tq + row) >> 6)
    qi_all = jnp.concatenate([qi_ref[:, h * IDX_DIM:(h + 1) * IDX_DIM] for h in range(IDX_HEADS)], axis=0)
    for c in range(nk // span):
        ks = slice(c * span, (c + 1) * span)
        rel = jnp.maximum(_dot_nt(qi_all, ki_ref[ks, :IDX_DIM]), 0.0)
        score = jnp.zeros((tq, span), F32)
        for h in range(IDX_HEADS):
            score = score + rel[h * tq:(h + 1) * tq] * w[:, W_IDX_LANE + h:W_IDX_LANE + h + 1]
        if ks == last:
            score = jnp.where(allowed, score, NEG_INF)
        bits = lax.bitcast_convert_type(score, jnp.int32)
        key_ref[:, ks] = bits ^ ((bits >> 31) & 0x7FFFFFFF)

    groups = [slice(r * DSA_ROW_GROUP, (r + 1) * DSA_ROW_GROUP) for r in range(tq // DSA_ROW_GROUP)]

    def select_bit(j, ts):
        bit = lax.shift_left(jnp.int32(1), 31 - j)
        out = []
        for rows, t in zip(groups, ts):
            cand = t | bit
            cnt = jnp.sum(jnp.where(key_ref[rows, :nk] >= (cand ^ INT_MIN), 1.0, 0.0), axis=1, keepdims=True)
            out.append(jnp.where(cnt >= topk, cand, t))
        return tuple(out)

    ts = lax.fori_loop(0, 32, select_bit, tuple(jnp.zeros((DSA_ROW_GROUP, 1), jnp.int32) for _ in groups))
    thr = jnp.maximum(jnp.concatenate(ts, axis=0) ^ INT_MIN, INT_MIN + 1)
    key_ref[:, last] = jnp.where(allowed, key_ref[:, last], INT_MIN)

    k = k_ref[:nk, :]
    v = v_ref[:nk, :]
    for g in range(N_HEADS_A // DSA_HEAD_GROUP):
        heads = [slice(h * HEAD_DIM, (h + 1) * HEAD_DIM) for h in range(g * DSA_HEAD_GROUP, (g + 1) * DSA_HEAD_GROUP)]
        lg = _dot_nt(jnp.concatenate([q_ref[:, cols] for cols in heads], axis=0), k) * SCALE
        selected = key_ref[:, :nk] >= thr
        lg = jnp.concatenate([jnp.where(selected, lg[j * tq:(j + 1) * tq], NEG_INF) for j in range(len(heads))], axis=0)
        e = jnp.exp(lg - jnp.max(lg, axis=1, keepdims=True))
        s = jnp.sum(e, axis=1, keepdims=True)
        o = _dot(e.astype(BF16), v) / s
        for j, cols in enumerate(heads):
            o_ref[:, cols] = o[j * tq:(j + 1) * tq].astype(BF16)


def _dsa_attention(qkv_a, qk_i, w_i, batch, seq):
    tq = 2 * CHUNK
    nq = seq // tq
    hq = N_HEADS_A * HEAD_DIM
    kcol = hq // LANE
    icol = (IDX_HEADS * IDX_DIM) // LANE
    qmap = lambda b, i: (b * nq + i, 0)
    return pl.pallas_call(
        functools.partial(_dsa_kernel, tq=tq, seq=seq, topk=min(TOPK_MAX, seq // 4)),
        grid=(batch, nq),
        in_specs=[pl.BlockSpec((tq, hq), qmap),
                  pl.BlockSpec((seq, LANE), lambda b, i: (b, kcol)),
                  pl.BlockSpec((seq, LANE), lambda b, i: (b, kcol + 1)),
                  pl.BlockSpec((tq, IDX_HEADS * IDX_DIM), qmap),
                  pl.BlockSpec((seq, LANE), lambda b, i: (b, icol)),
                  pl.BlockSpec((tq, LANE), qmap)],
        out_specs=pl.BlockSpec((tq, hq), qmap),
        out_shape=jax.ShapeDtypeStruct((batch * seq, hq), BF16),
        scratch_shapes=[pltpu.VMEM((tq, seq), jnp.int32)],
        compiler_params=_cparams(("parallel", "parallel"), 48),
        name="dsa_attention",
    )(qkv_a, qkv_a, qkv_a, qk_i, qk_i, w_i)


def _band_bias_kernel(rel_ref, o_ref):
    h = pl.program_id(0)
    i = lax.broadcasted_iota(jnp.int32, (BAND_Q, BAND_Q), 0)
    jj = lax.broadcasted_iota(jnp.int32, (BAND_Q, BAND_Q), 1)
    for jb in range(BAND_KB):
        j = jb * BAND_Q + jj
        idx = jnp.clip(i - j + LEFT_CHUNKS * CHUNK, -(CHUNK - 1), REL_CLIP) + (CHUNK - 1)
        cq = i >> 6
        ck = j >> 6
        d_lo = LEFT_CHUNKS * CHUNK - (jb + 1) * BAND_Q + 1
        d_hi = LEFT_CHUNKS * CHUNK - jb * BAND_Q + BAND_Q - 1
        r_lo = min(max(d_lo, -(CHUNK - 1)), REL_CLIP) + (CHUNK - 1)
        r_hi = min(max(d_hi, -(CHUNK - 1)), REL_CLIP) + (CHUNK - 1)
        val = lax.fori_loop(r_lo, r_hi + 1, lambda r, acc: jnp.where(idx == r, rel_ref[h, r], acc),
                            jnp.zeros((BAND_Q, BAND_Q), F32))
        val = jnp.where(ck >= cq, val, NEG_INF)
        o_ref[0, jb] = jnp.where(ck <= cq + LEFT_CHUNKS, val, NEG_INF)


def _band_bias(rel_bias):
    nh = rel_bias.shape[0]
    return pl.pallas_call(
        _band_bias_kernel,
        grid=(nh,),
        in_specs=[pl.BlockSpec(memory_space=pltpu.SMEM)],
        out_specs=pl.BlockSpec((1, BAND_KB, BAND_Q, BAND_Q), lambda h: (h, 0, 0, 0)),
        out_shape=jax.ShapeDtypeStruct((nh, BAND_KB, BAND_Q, BAND_Q), F32),
        name="band_bias",
    )(rel_bias)


def _band_kernel(q_ref, k_ref, v_ref, bias_ref, o_ref):
    p = pl.program_id(1)
    first = BAND_KB - 1
    shift = jnp.maximum(first - p, 0)
    s0 = pl.multiple_of(jnp.maximum(p - first, 0) * BAND_Q, BAND_Q)
    heads = [slice(h * HEAD_DIM, (h + 1) * HEAD_DIM) for h in range(N_HEADS_B)]
    logits = [_dot_nt(q_ref[:, cols], k_ref[pl.ds(s0, BAND_W), cols]) * SCALE for cols in heads]
    probs, sums = [], []
    for h, lg in enumerate(logits):
        blocks = []
        for jb in range(BAND_KB):
            src = jb + shift
            blk = lg[:, jb * BAND_Q:(jb + 1) * BAND_Q] + bias_ref[h, jnp.minimum(src, first)]
            blocks.append(jnp.where(src <= first, blk, NEG_INF))
        lg = jnp.concatenate(blocks, axis=1)
        e = jnp.exp(lg - jnp.max(lg, axis=1, keepdims=True))
        sums.append(jnp.sum(e, axis=1, keepdims=True))
        probs.append(e.astype(BF16))
    outs = [_dot(e, v_ref[pl.ds(s0, BAND_W), cols]) for e, cols in zip(probs, heads)]
    for o, s, cols in zip(outs, sums, heads):
        o_ref[:, cols] = (o / s).astype(BF16)


def _band_attention(qkv_b, bias, batch, seq):
    nq = seq // BAND_Q
    hq = N_HEADS_B * HEAD_DIM
    qmap = lambda b, p: (b * nq + p, 0)
    return pl.pallas_call(
        _band_kernel,
        grid=(batch, nq),
        in_specs=[pl.BlockSpec((BAND_Q, hq), qmap),
                  pl.BlockSpec((seq, hq), lambda b, p: (b, 1)),
                  pl.BlockSpec((seq, hq), lambda b, p: (b, 2)),
                  pl.BlockSpec(bias.shape, lambda b, p: (0, 0, 0, 0))],
        out_specs=pl.BlockSpec((BAND_Q, hq), qmap),
        out_shape=jax.ShapeDtypeStruct((batch * seq, hq), BF16),
        compiler_params=_cparams(("parallel", "parallel"), 48),
        name="band_attention",
    )(qkv_b, qkv_b, qkv_b, bias)


def _sb_blocks(qs, k_blks, v_blks, later, carry, past):
    zs = [_dot_nt(q, k) * (SCALE * LOG2_E) for q, k in zip(qs, k_blks)]
    log_betas, drops, splits = [], [], []
    for z2 in zs:
        neg_abs = lax.bitcast_convert_type(lax.bitcast_convert_type(z2, jnp.int32) | INT_MIN, F32)
        drop = jnp.maximum(z2, 0.0) + jnp.log2(1.0 + jnp.exp2(neg_abs))
        log_betas.append(z2 - drop)
        if past is not None:
            drop = jnp.where(past, drop, 0.0)
        drops.append(drop)
        hi = drop.astype(BF16)
        r1 = drop - hi.astype(F32)
        mid = r1.astype(BF16)
        lo = (r1 - mid.astype(F32)).astype(BF16)
        splits.append(jnp.concatenate([hi, mid, lo], axis=1))
    afters = [_dot(s, later) for s in splits]
    probs, tails = [], []
    for h, (log_beta, drop, after) in enumerate(zip(log_betas, drops, afters)):
        row_sum = jnp.sum(drop, axis=1, keepdims=True)
        if carry is None:
            a = jnp.exp2(log_beta - after)
            tails.append(row_sum)
        else:
            a = jnp.exp2(log_beta - carry[2 * h] - after)
            tails.append(carry[2 * h] + row_sum)
        if past is not None:
            a = jnp.where(past, a, 0.0)
        probs.append(a.astype(BF16))
    pvs = [_dot(a, v) for a, v in zip(probs, v_blks)]
    out = []
    for h, (tail, pv) in enumerate(zip(tails, pvs)):
        out += [tail, pv if carry is None else carry[2 * h + 1] + pv]
    return tuple(out)


def _sb_kernel(q_ref, k_ref, v_ref, o_ref, *, tq, nh):
    i = pl.program_id(2)
    r3 = lax.broadcasted_iota(jnp.int32, (3 * tq, tq), 0)
    c3 = lax.broadcasted_iota(jnp.int32, (3 * tq, tq), 1)
    later = jnp.where((r3 & (tq - 1)) > c3, 1.0, 0.0).astype(BF16)
    row = lax.broadcasted_iota(jnp.int32, (tq, tq), 0)
    col = lax.broadcasted_iota(jnp.int32, (tq, tq), 1)
    heads = [slice(h * HEAD_DIM, (h + 1) * HEAD_DIM) for h in range(nh)]

    def walk(s0, carry, past):
        return _sb_blocks([q_ref[:, cols] for cols in heads],
                          [k_ref[pl.ds(s0, tq), cols] for cols in heads],
                          [v_ref[pl.ds(s0, tq), cols] for cols in heads], later, carry, past)

    def least_tail(c):
        m = c[0]
        for h in range(1, nh):
            m = jnp.minimum(m, c[2 * h])
        return jnp.min(m)

    def alive(state):
        return jnp.logical_and(state[0] < i, state[1] < SB_DEAD_BITS)

    def step(state):
        c = walk(pl.multiple_of((i - 1 - state[0]) * tq, tq), state[2:], None)
        return (state[0] + 1, least_tail(c)) + c

    carry = walk(pl.multiple_of(i * tq, tq), None, col < row)
    carry = lax.while_loop(alive, step, (jnp.int32(0), least_tail(carry)) + carry)[2:]
    for h, cols in enumerate(heads):
        o_ref[:, cols] = carry[2 * h + 1].astype(BF16)


def _stick_breaking(qkv, batch, seq, tq, nh):
    nq = seq // tq
    ng = N_HEADS_C // nh
    w = nh * HEAD_DIM
    qmap = lambda b, g, i: (b * nq + i, g)
    return pl.pallas_call(
        functools.partial(_sb_kernel, tq=tq, nh=nh),
        grid=(batch, ng, nq),
        in_specs=[pl.BlockSpec((tq, w), qmap),
                  pl.BlockSpec((seq, w), lambda b, g, i: (b, ng + g)),
                  pl.BlockSpec((seq, w), lambda b, g, i: (b, 2 * ng + g))],
        out_specs=pl.BlockSpec((tq, w), qmap),
        out_shape=jax.ShapeDtypeStruct((batch * seq, N_HEADS_C * HEAD_DIM), BF16),
        compiler_params=_cparams(("parallel", "parallel", "parallel"), 32),
        name="stick_breaking",
    )(qkv, qkv, qkv)


def _xattn_kernel(xb_ref, xf_ref, kv_ref, wq_ref, wo_ref, g_ref, b_ref, of_ref, ob_ref):
    q = _dot(xb_ref[...], wq_ref[...]).astype(BF16)
    hk = N_HEADS_X * HEAD_DIM
    heads = [slice(h * HEAD_DIM, (h + 1) * HEAD_DIM) for h in range(N_HEADS_X)]
    logits = [_dot_nt(q[:, cols], kv_ref[:, cols]) * SCALE for cols in heads]
    probs, sums = [], []
    for lg in logits:
        e = jnp.exp(lg - jnp.max(lg, axis=1, keepdims=True))
        sums.append(jnp.sum(e, axis=1, keepdims=True))
        probs.append(e.astype(BF16))
    outs = [_dot(e, kv_ref[:, hk + h * HEAD_DIM:hk + (h + 1) * HEAD_DIM]) for h, e in enumerate(probs)]
    o = jnp.concatenate([(oh / s).astype(BF16) for oh, s in zip(outs, sums)], axis=1)
    y = _layer_norm(ALPHA * xf_ref[...] + _dot(o, wo_ref[...]), g_ref[...], b_ref[...])
    of_ref[...] = y
    ob_ref[...] = y.astype(BF16)


def _xattn(xb, xf, kv, wq, wo, g, b, seq, tm):
    m, n = xf.shape
    per_seq = seq // tm
    row = lambda i: (i, 0)
    const = lambda i: (0, 0)
    return pl.pallas_call(
        _xattn_kernel,
        grid=(m // tm,),
        in_specs=[pl.BlockSpec((tm, n), row), pl.BlockSpec((tm, n), row),
                  pl.BlockSpec((MEM_LEN, kv.shape[1]), lambda i: (i // per_seq, 0)),
                  pl.BlockSpec(wq.shape, const), pl.BlockSpec(wo.shape, const),
                  pl.BlockSpec((1, n), const), pl.BlockSpec((1, n), const)],
        out_specs=[pl.BlockSpec((tm, n), row), pl.BlockSpec((tm, n), row)],
        out_shape=[jax.ShapeDtypeStruct((m, n), F32), jax.ShapeDtypeStruct((m, n), BF16)],
        compiler_params=_cparams(("parallel",), 56),
        name="xattn_ln",
    )(xb, xf, kv, wq, wo, g, b)


def _rope_tables(seq):
    pos = jnp.arange(seq).astype(F32)[:, None]

    def half_tables(half):
        inv_freq = ROPE_THETA ** (-jnp.arange(half, dtype=F32) / half)
        ang = pos * inv_freq[None, :]
        return jnp.cos(ang), jnp.sin(ang)

    c64, s64 = half_tables(HEAD_DIM // 2)
    c32, s32 = half_tables(IDX_DIM // 2)
    head = (jnp.concatenate([c64, c64], 1), jnp.concatenate([-s64, s64], 1))
    idx = (jnp.concatenate([c32, c32, c32, c32], 1), jnp.concatenate([-s32, s32, -s32, s32], 1))
    rest = LANE - IDX_DIM
    idx_key = (jnp.concatenate([c32, c32, jnp.ones((seq, rest), F32)], 1),
               jnp.concatenate([-s32, s32, jnp.zeros((seq, rest), F32)], 1))
    return head, idx, idx_key


def kernel(x, mem, ln_g, ln_b, ffn_in, ffn_out, xattn_q, xattn_kv, xattn_o,
           even_in, even_out, even_rel_bias, odd_in, odd_out):
    batch, seq, d = x.shape
    t = batch * seq
    xf = x.reshape(t, d)
    xb = xf.astype(BF16)
    memb = mem.reshape(batch * MEM_LEN, d).astype(BF16)
    tab_head, tab_idx, tab_idx_key = _rope_tables(seq)

    modes_a = ((0, HEAD_DIM),) * (N_HEADS_A + 1) + (None,)
    modes_i = ((0, IDX_DIM),) * ((IDX_HEADS * IDX_DIM) // LANE) + ((1, IDX_DIM),)

    def ln_params(layer, j):
        return ln_g[layer, j][None, :], ln_b[layer, j][None, :]

    ffn_out_b = ffn_out.astype(BF16)

    def ffn(xf, xb, layer, j):
        g, b = ln_params(layer, 3 * j)
        h = _ffn_in(xb, ffn_in, (layer, j), 1024, 512)
        return _ffn_out(h, ffn_out_b, (layer, j), xf, g, b, 1024, FFN_DIM // 11, 0.5)

    for layer in range(DEPTH):
        xf, xb = ffn(xf, xb, layer, 0)

        g, b = ln_params(layer, 1)
        if layer % 2 == 0:
            w_in = even_in[layer // 2]
            w_a = w_in[:, :A_COLS].astype(BF16)
            w_i = jnp.pad(w_in[:, A_COLS:A_COLS + I_COLS], ((0, 0), (0, I_PAD - I_COLS))).astype(BF16)
            w_b = w_in[:, A_COLS + I_COLS:].astype(BF16)
            qkv_a = _proj_rope(xb, w_a, [tab_head], modes_a, seq, 512, False, "proj_dsa")[0]
            qk_i, w_idx = _proj_rope(xb, w_i, [tab_idx, tab_idx_key], modes_i, seq, 512, True, "proj_idx")
            qkv_b = _matmul(xb, w_b, 1024, 1024, "proj_band")
            o_a = _dsa_attention(qkv_a, qk_i, w_idx, batch, seq)
            o_b = _band_attention(qkv_b, _band_bias(even_rel_bias[layer // 2]), batch, seq)
            xf, xb = _out_ln([o_a, o_b], even_out[layer // 2].astype(BF16), xf, g, b, 512, "even_out_ln")
        else:
            qkv = _matmul(xb, odd_in, 1024, 1024, "proj_odd", lead=(layer // 2,))
            o = _stick_breaking(qkv, batch, seq, 256, 4)
            xf, xb = _out_ln([o], odd_out[layer // 2].astype(BF16), xf, g, b, 512, "odd_out_ln")

        g, b = ln_params(layer, 2)
        kv = _matmul(memb, xattn_kv[layer].astype(BF16), 1024, 1024, "proj_mem")
        xf, xb = _xattn(xb, xf, kv, xattn_q[layer].astype(BF16), xattn_o[layer].astype(BF16), g, b, seq, 512)

        xf, xb = ffn(xf, xb, layer, 1)

    return xf.reshape(batch, seq, d)
```

```python
import functools

import jax
import jax.numpy as jnp
from jax import lax
from jax.experimental import pallas as pl
from jax.experimental.pallas import tpu as pltpu

F32 = jnp.float32
BF16 = jnp.bfloat16

D_MODEL = 2048
DEPTH = 4
CHUNK = 64
MEM_LEN = 256
HEAD_DIM = 128
ROPE_THETA = 10000.0
LN_EPS = 1e-5
NEG_INF = -1e30
N_HEADS_A = 8
N_HEADS_B = 8
IDX_HEADS = 16
IDX_DIM = 64
TOPK_MAX = 256
LEFT_CHUNKS = 8
REL_CLIP = 128
REL_SIZE = CHUNK + REL_CLIP
N_HEADS_C = 16
N_HEADS_X = 4
FFN_DIM = ((8 * D_MODEL // 3 + 255) // 256) * 256
ALPHA = (2.0 * DEPTH) ** 0.25
SCALE = HEAD_DIM ** -0.5
LOG2_E = 1.4426950408889634

LANE = 128
INT_MIN = -2147483648
MIB = 1024 * 1024

A_COLS = (N_HEADS_A + 2) * HEAD_DIM
I_COLS = IDX_HEADS * IDX_DIM + IDX_DIM + IDX_HEADS
I_PAD = ((I_COLS + LANE - 1) // LANE) * LANE
W_IDX_LANE = IDX_DIM
B_COLS = 3 * N_HEADS_B * HEAD_DIM

BAND_Q = 2 * CHUNK
BAND_KB = (LEFT_CHUNKS * CHUNK) // BAND_Q + 1
BAND_W = BAND_KB * BAND_Q

LN_SUB_ROWS = 256

SB_DEAD_BITS = 152.0

DSA_KEY_SPAN = 512
DSA_HEAD_GROUP = 4
BIT_GROUP = 32 * 8
DSA_ROW_GROUP = 32


def _cparams(sem, vmem_mib):
    return pltpu.CompilerParams(dimension_semantics=sem, vmem_limit_bytes=vmem_mib * MIB)


def _dot(a, b):
    return jnp.dot(a, b, preferred_element_type=F32)


def _dot_nt(a, b):
    return lax.dot_general(a, b, (((1,), (1,)), ((), ())), preferred_element_type=F32)


def _layer_norm(y, g, b):
    mu = jnp.mean(y, axis=-1, keepdims=True)
    d = y - mu
    var = jnp.mean(d * d, axis=-1, keepdims=True)
    return d * lax.rsqrt(var + LN_EPS) * g + b


def _mm_kernel(x_ref, w_ref, o_ref, *scratch):
    if scratch:
        wb_ref, = scratch

        @pl.when(pl.program_id(1) == 0)
        def _():
            wb_ref[...] = w_ref[...].astype(BF16)
    else:
        wb_ref = w_ref
    o_ref[...] = _dot(x_ref[...], wb_ref[...]).astype(o_ref.dtype)


def _matmul(x, w, tm, tn, name, lead=()):
    m, k = x.shape
    n = w.shape[-1]
    cast = w.dtype != BF16
    return pl.pallas_call(
        _mm_kernel,
        grid=(n // tn, m // tm),
        in_specs=[pl.BlockSpec((tm, k), lambda j, i: (i, 0)),
                  pl.BlockSpec((None,) * len(lead) + (k, tn), lambda j, i: lead + (0, j))],
        out_specs=pl.BlockSpec((tm, tn), lambda j, i: (i, j)),
        out_shape=jax.ShapeDtypeStruct((m, n), BF16),
        scratch_shapes=[pltpu.VMEM((k, tn), BF16)] if cast else [],
        compiler_params=_cparams(("parallel", "arbitrary"), 48),
        name=name,
    )(x, w)


def _rope_group(x, cos, sin, width):
    if width == LANE:
        partner = pltpu.roll(x, LANE // 2, 1)
    else:
        lane = lax.broadcasted_iota(jnp.int32, x.shape, 1)
        half = width // 2
        partner = jnp.where((lane & half) == 0, pltpu.roll(x, LANE - half, 1), pltpu.roll(x, half, 1))
    return x * cos + partner * sin


def _proj_rope_kernel(x_ref, w_ref, *rest, modes, n_tab, f32_tail):
    tabs = rest[:2 * n_tab]
    outs = rest[2 * n_tab:]
    acc = _dot(x_ref[...], w_ref[...])
    for g, mode in enumerate(modes):
        blk = acc[:, g * LANE:(g + 1) * LANE]
        if mode is not None:
            tid, width = mode
            blk = _rope_group(blk, tabs[2 * tid][...], tabs[2 * tid + 1][...], width)
        outs[0][:, g * LANE:(g + 1) * LANE] = blk.astype(BF16)
    if f32_tail:
        outs[1][...] = acc[:, -LANE:]


def _proj_rope(x, w, tabs, modes, seq, tm, f32_tail, name):
    m, k = x.shape
    n = w.shape[1]
    per_seq = seq // tm
    in_specs = [pl.BlockSpec((tm, k), lambda i: (i, 0)), pl.BlockSpec((k, n), lambda i: (0, 0))]
    args = [x, w]
    for cos, sin in tabs:
        in_specs += [pl.BlockSpec((tm, LANE), lambda i: (i % per_seq, 0))] * 2
        args += [cos, sin]
    out_specs = [pl.BlockSpec((tm, n), lambda i: (i, 0))]
    out_shape = [jax.ShapeDtypeStruct((m, n), BF16)]
    if f32_tail:
        out_specs.append(pl.BlockSpec((tm, LANE), lambda i: (i, 0)))
        out_shape.append(jax.ShapeDtypeStruct((m, LANE), F32))
    return pl.pallas_call(
        functools.partial(_proj_rope_kernel, modes=modes, n_tab=len(tabs), f32_tail=f32_tail),
        grid=(m // tm,),
        in_specs=in_specs,
        out_specs=out_specs,
        out_shape=out_shape,
        compiler_params=_cparams(("parallel",), 48),
        name=name,
    )(*args)


def _ffn_in_kernel(x_ref, wa_ref, wg_ref, o_ref, wab_ref, wgb_ref):
    @pl.when(pl.program_id(1) == 0)
    def _():
        wab_ref[...] = wa_ref[...].astype(BF16)
        wgb_ref[...] = wg_ref[...].astype(BF16)

    x = x_ref[...]
    a = _dot(x, wab_ref[...])
    g = _dot(x, wgb_ref[...])
    o_ref[...] = (a * jax.nn.sigmoid(a) * g).astype(o_ref.dtype)


def _ffn_in(xb, w, lead, tm, tf):
    m, k = xb.shape
    nf = FFN_DIM // tf
    wblock = (None,) * len(lead) + (k, tf)
    return pl.pallas_call(
        _ffn_in_kernel,
        grid=(nf, m // tm),
        in_specs=[pl.BlockSpec((tm, k), lambda j, i: (i, 0)),
                  pl.BlockSpec(wblock, lambda j, i: lead + (0, j)),
                  pl.BlockSpec(wblock, lambda j, i: lead + (0, j + nf))],
        out_specs=pl.BlockSpec((tm, tf), lambda j, i: (i, j)),
        out_shape=jax.ShapeDtypeStruct((m, FFN_DIM), BF16),
        scratch_shapes=[pltpu.VMEM((k, tf), BF16), pltpu.VMEM((k, tf), BF16)],
        compiler_params=_cparams(("parallel", "arbitrary"), 56),
        name="ffn_in",
    )(xb, w, w)


def _ffn_out_kernel(h_ref, w_ref, x_ref, g_ref, b_ref, of_ref, ob_ref, *, nk, scale):
    k = pl.program_id(1)

    @pl.when(k == 0)
    def _():
        of_ref[...] = _dot(h_ref[...], w_ref[...].astype(BF16))

    @pl.when(jnp.logical_and(k > 0, k < nk - 1))
    def _():
        of_ref[...] += _dot(h_ref[...], w_ref[...].astype(BF16))

    @pl.when(k == nk - 1)
    def _():
        w = w_ref[...].astype(BF16)
        for r in range(0, of_ref.shape[0], LN_SUB_ROWS):
            rows = slice(r, r + LN_SUB_ROWS)
            acc = of_ref[rows, :] + _dot(h_ref[rows, :], w)
            y = _layer_norm(ALPHA * x_ref[rows, :] + scale * acc, g_ref[...], b_ref[...])
            of_ref[rows, :] = y
            ob_ref[rows, :] = y.astype(BF16)


def _ffn_out(h, w, lead, xf, g, b, tm, tk, scale):
    m, kdim = h.shape
    n = w.shape[-1]
    nk = kdim // tk
    row = lambda i, k: (i, 0)
    return pl.pallas_call(
        functools.partial(_ffn_out_kernel, nk=nk, scale=scale),
        grid=(m // tm, nk),
        in_specs=[pl.BlockSpec((tm, tk), lambda i, k: (i, k)),
                  pl.BlockSpec((None,) * len(lead) + (tk, n), lambda i, k: lead + (k, 0)),
                  pl.BlockSpec((tm, n), row),
                  pl.BlockSpec((1, n), lambda i, k: (0, 0)),
                  pl.BlockSpec((1, n), lambda i, k: (0, 0))],
        out_specs=[pl.BlockSpec((tm, n), row), pl.BlockSpec((tm, n), row)],
        out_shape=[jax.ShapeDtypeStruct((m, n), F32), jax.ShapeDtypeStruct((m, n), BF16)],
        compiler_params=_cparams(("parallel", "arbitrary"), 60),
        name="ffn_out_ln",
    )(h, w, xf, g, b)


def _out_ln_kernel(*refs, n_lhs):
    lhs = refs[:n_lhs]
    ws = refs[n_lhs:2 * n_lhs]
    x_ref, g_ref, b_ref, of_ref, ob_ref = refs[2 * n_lhs:]
    for r in range(0, of_ref.shape[0], LN_SUB_ROWS):
        rows = slice(r, r + LN_SUB_ROWS)
        acc = _dot(lhs[0][rows, :], ws[0][...])
        for l_ref, w_ref in zip(lhs[1:], ws[1:]):
            acc = acc + _dot(l_ref[rows, :], w_ref[...])
        y = _layer_norm(ALPHA * x_ref[rows, :] + acc, g_ref[...], b_ref[...])
        of_ref[rows, :] = y
        ob_ref[rows, :] = y.astype(BF16)


def _out_ln(lhs_list, w, xf, g, b, tm, name):
    m, n = xf.shape
    row = lambda i: (i, 0)
    in_specs, args, off = [], [], 0
    for l in lhs_list:
        in_specs.append(pl.BlockSpec((tm, l.shape[1]), row))
        args.append(l)
    for l in lhs_list:
        kl = l.shape[1]
        in_specs.append(pl.BlockSpec((kl, n), functools.partial(lambda i, o: (o, 0), o=off // kl)))
        args.append(w)
        off += kl
    in_specs += [pl.BlockSpec((tm, n), row), pl.BlockSpec((1, n), lambda i: (0, 0)), pl.BlockSpec((1, n), lambda i: (0, 0))]
    args += [xf, g, b]
    return pl.pallas_call(
        functools.partial(_out_ln_kernel, n_lhs=len(lhs_list)),
        grid=(m // tm,),
        in_specs=in_specs,
        out_specs=[pl.BlockSpec((tm, n), row), pl.BlockSpec((tm, n), row)],
        out_shape=[jax.ShapeDtypeStruct((m, n), F32), jax.ShapeDtypeStruct((m, n), BF16)],
        compiler_params=_cparams(("parallel",), 56),
        name=name,
    )(*args)


def _dsa_kernel(q_ref, k_ref, v_ref, qi_ref, ki_ref, w_ref, o_ref, plane_ref, bias_ref, *, tq, seq, topk):
    i = pl.program_id(1)
    per_span = DSA_KEY_SPAN // tq
    for n in range(1, seq // DSA_KEY_SPAN + 1):
        pl.when(i // per_span == n - 1)(functools.partial(
            _dsa_body, i, q_ref, k_ref, v_ref, qi_ref, ki_ref, w_ref, o_ref, plane_ref, bias_ref,
            tq=tq, nk=n * DSA_KEY_SPAN, topk=topk))


def _dsa_body(i, q_ref, k_ref, v_ref, qi_ref, ki_ref, w_ref, o_ref, plane_ref, bias_ref, *, tq, nk, topk):
    span = DSA_KEY_SPAN
    n_groups = nk // BIT_GROUP
    w_t = w_ref[...].T
    key_pos = nk - span + lax.broadcasted_iota(jnp.int32, (span, tq), 0)
    q_pos = i * tq + lax.broadcasted_iota(jnp.int32, (span, tq), 1)
    allowed = (key_pos >> 6) <= (q_pos >> 6)
    qi_all = jnp.concatenate([qi_ref[:, h * IDX_DIM:(h + 1) * IDX_DIM] for h in range(IDX_HEADS)], axis=0)
    for c in range(nk // span):
        rel = jnp.maximum(_dot_nt(ki_ref[c * span:(c + 1) * span, :IDX_DIM], qi_all), 0.0)
        score = jnp.zeros((span, tq), F32)
        for h in range(IDX_HEADS):
            score = score + rel[:, h * tq:(h + 1) * tq] * w_t[W_IDX_LANE + h:W_IDX_LANE + h + 1, :]
        if c == nk // span - 1:
            score = jnp.where(allowed, score, NEG_INF)
        bits = lax.bitcast_convert_type(score, jnp.int32)
        u = bits ^ ((bits >> 31) | INT_MIN)
        for gg in range(span // BIT_GROUP):
            words = [u[gg * BIT_GROUP + 8 * j:gg * BIT_GROUP + 8 * j + 8, :] for j in range(32)]
            g = c * (span // BIT_GROUP) + gg
            for b, plane in enumerate(_bit_transpose32(words)):
                plane_ref[g, b] = plane

    def select_bit(j, state):
        alive, keep, above = state[:n_groups], state[n_groups:2 * n_groups], state[-1]
        ones = [a & plane_ref[g, 31 - j] for g, a in enumerate(alive)]
        cnt = lax.population_count(ones[0])
        for o in ones[1:]:
            cnt = cnt + lax.population_count(o)
        for shift in (4, 2, 1):
            cnt = cnt + pltpu.roll(cnt, shift, 0)
        take = (above + cnt) >= topk
        zeros = [a ^ o for a, o in zip(alive, ones)]
        new_alive = [jnp.where(take, o, z) for o, z in zip(ones, zeros)]
        new_keep = [jnp.where(take, kp ^ z, kp) for kp, z in zip(keep, zeros)]
        return tuple(new_alive) + tuple(new_keep) + (jnp.where(take, above, above + cnt),)

    everything = jnp.full((8, tq), -1, jnp.int32)
    state = lax.fori_loop(0, 32, select_bit, (everything,) * (2 * n_groups) + (jnp.zeros((8, tq), jnp.int32),))
    keep = state[n_groups:2 * n_groups]
    for g in range(n_groups):
        for j in range(32):
            rows = slice(g * BIT_GROUP + 8 * j, g * BIT_GROUP + 8 * j + 8)
            bias = jnp.where(lax.shift_left(keep[g], 31 - j) < 0, 0.0, NEG_INF)
            if g * BIT_GROUP + 8 * j >= nk - span:
                bias = jnp.where(allowed[rows.start - (nk - span):rows.stop - (nk - span), :], bias, NEG_INF)
            bias_ref[rows, :] = bias

    k = k_ref[:nk, :]
    v_ext = jnp.concatenate([v_ref[:nk, :], jnp.ones((nk, HEAD_DIM), BF16)], axis=1)
    for g in range(N_HEADS_A // DSA_HEAD_GROUP):
        heads = [slice(h * HEAD_DIM, (h + 1) * HEAD_DIM) for h in range(g * DSA_HEAD_GROUP, (g + 1) * DSA_HEAD_GROUP)]
        lg = _dot_nt(k, jnp.concatenate([q_ref[:, cols] for cols in heads], axis=0)) * SCALE
        lg = jnp.concatenate([lg[:, j * tq:(j + 1) * tq] + bias_ref[:nk, :] for j in range(len(heads))], axis=1)
        e = jnp.exp(lg - jnp.max(lg, axis=0, keepdims=True)).astype(BF16)
        o = lax.dot_general(e, v_ext, (((0,), (0,)), ((), ())), preferred_element_type=F32)
        o = o[:, :HEAD_DIM] / o[:, HEAD_DIM:HEAD_DIM + 1]
        for j, cols in enumerate(heads):
            o_ref[:, cols] = o[j * tq:(j + 1) * tq].astype(BF16)


def _bit_transpose32(words):
    words = list(words)
    for dist, low in ((16, 0x0000FFFF), (8, 0x00FF00FF), (4, 0x0F0F0F0F), (2, 0x33333333), (1, 0x55555555)):
        for k in range(32):
            if k & dist:
                continue
            a, b = words[k], words[k + dist]
            t = (lax.shift_right_logical(a, dist) ^ b) & low
            words[k + dist] = b ^ t
            words[k] = a ^ lax.shift_left(t, dist)
    return words


def _dsa_attention(qkv_a, qk_i, w_i, batch, seq):
    tq = 2 * CHUNK
    nq = seq // tq
    hq = N_HEADS_A * HEAD_DIM
    kcol = hq // LANE
    icol = (IDX_HEADS * IDX_DIM) // LANE
    qmap = lambda b, i: (b * nq + i, 0)
    return pl.pallas_call(
        functools.partial(_dsa_kernel, tq=tq, seq=seq, topk=min(TOPK_MAX, seq // 4)),
        grid=(batch, nq),
        in_specs=[pl.BlockSpec((tq, hq), qmap),
                  pl.BlockSpec((seq, LANE), lambda b, i: (b, kcol)),
                  pl.BlockSpec((seq, LANE), lambda b, i: (b, kcol + 1)),
                  pl.BlockSpec((tq, IDX_HEADS * IDX_DIM), qmap),
                  pl.BlockSpec((seq, LANE), lambda b, i: (b, icol)),
                  pl.BlockSpec((tq, LANE), qmap)],
        out_specs=pl.BlockSpec((tq, hq), qmap),
        out_shape=jax.ShapeDtypeStruct((batch * seq, hq), BF16),
        scratch_shapes=[pltpu.VMEM((seq // BIT_GROUP, 32, 8, tq), jnp.int32), pltpu.VMEM((seq, tq), F32)],
        compiler_params=_cparams(("parallel", "parallel"), 48),
        name="dsa_attention",
    )(qkv_a, qkv_a, qkv_a, qk_i, qk_i, w_i)


def _band_bias_kernel(rel_ref, o_ref):
    h = pl.program_id(0)
    i = lax.broadcasted_iota(jnp.int32, (BAND_Q, BAND_Q), 0)
    jj = lax.broadcasted_iota(jnp.int32, (BAND_Q, BAND_Q), 1)
    for jb in range(BAND_KB):
        j = jb * BAND_Q + jj
        idx = jnp.clip(i - j + LEFT_CHUNKS * CHUNK, -(CHUNK - 1), REL_CLIP) + (CHUNK - 1)
        cq = i >> 6
        ck = j >> 6
        d_lo = LEFT_CHUNKS * CHUNK - (jb + 1) * BAND_Q + 1
        d_hi = LEFT_CHUNKS * CHUNK - jb * BAND_Q + BAND_Q - 1
        r_lo = min(max(d_lo, -(CHUNK - 1)), REL_CLIP) + (CHUNK - 1)
        r_hi = min(max(d_hi, -(CHUNK - 1)), REL_CLIP) + (CHUNK - 1)
        val = lax.fori_loop(r_lo, r_hi + 1, lambda r, acc: jnp.where(idx == r, rel_ref[h, r], acc),
                            jnp.zeros((BAND_Q, BAND_Q), F32))
        val = jnp.where(ck >= cq, val, NEG_INF)
        o_ref[0, jb] = jnp.where(ck <= cq + LEFT_CHUNKS, val, NEG_INF)


def _band_bias(rel_bias):
    nh = rel_bias.shape[0]
    return pl.pallas_call(
        _band_bias_kernel,
        grid=(nh,),
        in_specs=[pl.BlockSpec(memory_space=pltpu.SMEM)],
        out_specs=pl.BlockSpec((1, BAND_KB, BAND_Q, BAND_Q), lambda h: (h, 0, 0, 0)),
        out_shape=jax.ShapeDtypeStruct((nh, BAND_KB, BAND_Q, BAND_Q), F32),
        name="band_bias",
    )(rel_bias)


def _band_kernel(q_ref, k_ref, v_ref, bias_ref, o_ref):
    p = pl.program_id(1)
    first = BAND_KB - 1
    shift = jnp.maximum(first - p, 0)
    s0 = pl.multiple_of(jnp.maximum(p - first, 0) * BAND_Q, BAND_Q)
    heads = [slice(h * HEAD_DIM, (h + 1) * HEAD_DIM) for h in range(N_HEADS_B)]
    logits = [_dot_nt(q_ref[:, cols], k_ref[pl.ds(s0, BAND_W), cols]) * SCALE for cols in heads]
    probs, sums = [], []
    for h, lg in enumerate(logits):
        blocks = []
        for jb in range(BAND_KB):
            src = jb + shift
            blk = lg[:, jb * BAND_Q:(jb + 1) * BAND_Q] + bias_ref[h, jnp.minimum(src, first)]
            blocks.append(jnp.where(src <= first, blk, NEG_INF))
        lg = jnp.concatenate(blocks, axis=1)
        e = jnp.exp(lg - jnp.max(lg, axis=1, keepdims=True))
        sums.append(jnp.sum(e, axis=1, keepdims=True))
        probs.append(e.astype(BF16))
    outs = [_dot(e, v_ref[pl.ds(s0, BAND_W), cols]) for e, cols in zip(probs, heads)]
    for o, s, cols in zip(outs, sums, heads):
        o_ref[:, cols] = (o / s).astype(BF16)


def _band_attention(qkv_b, bias, batch, seq):
    nq = seq // BAND_Q
    hq = N_HEADS_B * HEAD_DIM
    qmap = lambda b, p: (b * nq + p, 0)
    return pl.pallas_call(
        _band_kernel,
        grid=(batch, nq),
        in_specs=[pl.BlockSpec((BAND_Q, hq), qmap),
                  pl.BlockSpec((seq, hq), lambda b, p: (b, 1)),
                  pl.BlockSpec((seq, hq), lambda b, p: (b, 2)),
                  pl.BlockSpec(bias.shape, lambda b, p: (0, 0, 0, 0))],
        out_specs=pl.BlockSpec((BAND_Q, hq), qmap),
        out_shape=jax.ShapeDtypeStruct((batch * seq, hq), BF16),
        compiler_params=_cparams(("parallel", "parallel"), 48),
        name="band_attention",
    )(qkv_b, qkv_b, qkv_b, bias)


def _sb_blocks(qs, k_blks, v_blks, later, carry, past):
    zs = [_dot_nt(q, k) * (SCALE * LOG2_E) for q, k in zip(qs, k_blks)]
    log_betas, drops, splits = [], [], []
    for z2 in zs:
        neg_abs = lax.bitcast_convert_type(lax.bitcast_convert_type(z2, jnp.int32) | INT_MIN, F32)
        drop = jnp.maximum(z2, 0.0) + jnp.log2(1.0 + jnp.exp2(neg_abs))
        log_betas.append(z2 - drop)
        if past is not None:
            drop = jnp.where(past, drop, 0.0)
        drops.append(drop)
        hi = drop.astype(BF16)
        r1 = drop - hi.astype(F32)
        mid = r1.astype(BF16)
        lo = (r1 - mid.astype(F32)).astype(BF16)
        splits.append(jnp.concatenate([hi, mid, lo], axis=1))
    afters = [_dot(s, later) for s in splits]
    probs, tails = [], []
    for h, (log_beta, drop, after) in enumerate(zip(log_betas, drops, afters)):
        row_sum = jnp.sum(drop, axis=1, keepdims=True)
        if carry is None:
            a = jnp.exp2(log_beta - after)
            tails.append(row_sum)
        else:
            a = jnp.exp2(log_beta - carry[2 * h] - after)
            tails.append(carry[2 * h] + row_sum)
        if past is not None:
            a = jnp.where(past, a, 0.0)
        probs.append(a.astype(BF16))
    pvs = [_dot(a, v) for a, v in zip(probs, v_blks)]
    out = []
    for h, (tail, pv) in enumerate(zip(tails, pvs)):
        out += [tail, pv if carry is None else carry[2 * h + 1] + pv]
    return tuple(out)


def _sb_kernel(q_ref, k_ref, v_ref, o_ref, *, tq, nh):
    i = pl.program_id(2)
    r3 = lax.broadcasted_iota(jnp.int32, (3 * tq, tq), 0)
    c3 = lax.broadcasted_iota(jnp.int32, (3 * tq, tq), 1)
    later = jnp.where((r3 & (tq - 1)) > c3, 1.0, 0.0).astype(BF16)
    row = lax.broadcasted_iota(jnp.int32, (tq, tq), 0)
    col = lax.broadcasted_iota(jnp.int32, (tq, tq), 1)
    heads = [slice(h * HEAD_DIM, (h + 1) * HEAD_DIM) for h in range(nh)]

    def walk(s0, carry, past):
        return _sb_blocks([q_ref[:, cols] for cols in heads],
                          [k_ref[pl.ds(s0, tq), cols] for cols in heads],
                          [v_ref[pl.ds(s0, tq), cols] for cols in heads], later, carry, past)

    def least_tail(c):
        m = c[0]
        for h in range(1, nh):
            m = jnp.minimum(m, c[2 * h])
        return jnp.min(m)

    def alive(state):
        return jnp.logical_and(state[0] < i, state[1] < SB_DEAD_BITS)

    def step(state):
        c = walk(pl.multiple_of((i - 1 - state[0]) * tq, tq), state[2:], None)
        return (state[0] + 1, least_tail(c)) + c

    carry = walk(pl.multiple_of(i * tq, tq), None, col < row)
    carry = lax.while_loop(alive, step, (jnp.int32(0), least_tail(carry)) + carry)[2:]
    for h, cols in enumerate(heads):
        o_ref[:, cols] = carry[2 * h + 1].astype(BF16)


def _stick_breaking(qkv, batch, seq, tq, nh):
    nq = seq // tq
    ng = N_HEADS_C // nh
    w = nh * HEAD_DIM
    qmap = lambda b, g, i: (b * nq + i, g)
    return pl.pallas_call(
        functools.partial(_sb_kernel, tq=tq, nh=nh),
        grid=(batch, ng, nq),
        in_specs=[pl.BlockSpec((tq, w), qmap),
                  pl.BlockSpec((seq, w), lambda b, g, i: (b, ng + g)),
                  pl.BlockSpec((seq, w), lambda b, g, i: (b, 2 * ng + g))],
        out_specs=pl.BlockSpec((tq, w), qmap),
        out_shape=jax.ShapeDtypeStruct((batch * seq, N_HEADS_C * HEAD_DIM), BF16),
        compiler_params=_cparams(("parallel", "parallel", "parallel"), 32),
        name="stick_breaking",
    )(qkv, qkv, qkv)


def _xattn_kernel(xb_ref, xf_ref, kv_ref, wq_ref, wo_ref, g_ref, b_ref, of_ref, ob_ref):
    q = _dot(xb_ref[...], wq_ref[...]).astype(BF16)
    hk = N_HEADS_X * HEAD_DIM
    heads = [slice(h * HEAD_DIM, (h + 1) * HEAD_DIM) for h in range(N_HEADS_X)]
    logits = [_dot_nt(q[:, cols], kv_ref[:, cols]) * SCALE for cols in heads]
    probs, sums = [], []
    for lg in logits:
        e = jnp.exp(lg - jnp.max(lg, axis=1, keepdims=True))
        sums.append(jnp.sum(e, axis=1, keepdims=True))
        probs.append(e.astype(BF16))
    outs = [_dot(e, kv_ref[:, hk + h * HEAD_DIM:hk + (h + 1) * HEAD_DIM]) for h, e in enumerate(probs)]
    o = jnp.concatenate([(oh / s).astype(BF16) for oh, s in zip(outs, sums)], axis=1)
    y = _layer_norm(ALPHA * xf_ref[...] + _dot(o, wo_ref[...]), g_ref[...], b_ref[...])
    of_ref[...] = y
    ob_ref[...] = y.astype(BF16)


def _xattn(xb, xf, kv, wq, wo, g, b, seq, tm):
    m, n = xf.shape
    per_seq = seq // tm
    row = lambda i: (i, 0)
    const = lambda i: (0, 0)
    return pl.pallas_call(
        _xattn_kernel,
        grid=(m // tm,),
        in_specs=[pl.BlockSpec((tm, n), row), pl.BlockSpec((tm, n), row),
                  pl.BlockSpec((MEM_LEN, kv.shape[1]), lambda i: (i // per_seq, 0)),
                  pl.BlockSpec(wq.shape, const), pl.BlockSpec(wo.shape, const),
                  pl.BlockSpec((1, n), const), pl.BlockSpec((1, n), const)],
        out_specs=[pl.BlockSpec((tm, n), row), pl.BlockSpec((tm, n), row)],
        out_shape=[jax.ShapeDtypeStruct((m, n), F32), jax.ShapeDtypeStruct((m, n), BF16)],
        compiler_params=_cparams(("parallel",), 56),
        name="xattn_ln",
    )(xb, xf, kv, wq, wo, g, b)


def _rope_tables(seq):
    pos = jnp.arange(seq).astype(F32)[:, None]

    def half_tables(half):
        inv_freq = ROPE_THETA ** (-jnp.arange(half, dtype=F32) / half)
        ang = pos * inv_freq[None, :]
        return jnp.cos(ang), jnp.sin(ang)

    c64, s64 = half_tables(HEAD_DIM // 2)
    c32, s32 = half_tables(IDX_DIM // 2)
    head = (jnp.concatenate([c64, c64], 1), jnp.concatenate([-s64, s64], 1))
    idx = (jnp.concatenate([c32, c32, c32, c32], 1), jnp.concatenate([-s32, s32, -s32, s32], 1))
    rest = LANE - IDX_DIM
    idx_key = (jnp.concatenate([c32, c32, jnp.ones((seq, rest), F32)], 1),
               jnp.concatenate([-s32, s32, jnp.zeros((seq, rest), F32)], 1))
    return head, idx, idx_key


def kernel(x, mem, ln_g, ln_b, ffn_in, ffn_out, xattn_q, xattn_kv, xattn_o,
           even_in, even_out, even_rel_bias, odd_in, odd_out):
    batch, seq, d = x.shape
    t = batch * seq
    xf = x.reshape(t, d)
    xb = xf.astype(BF16)
    memb = mem.reshape(batch * MEM_LEN, d).astype(BF16)
    tab_head, tab_idx, tab_idx_key = _rope_tables(seq)

    modes_a = ((0, HEAD_DIM),) * (N_HEADS_A + 1) + (None,)
    modes_i = ((0, IDX_DIM),) * ((IDX_HEADS * IDX_DIM) // LANE) + ((1, IDX_DIM),)

    def ln_params(layer, j):
        return ln_g[layer, j][None, :], ln_b[layer, j][None, :]

    def ffn(xf, xb, layer, j):
        g, b = ln_params(layer, 3 * j)
        h = _ffn_in(xb, ffn_in, (layer, j), 1024, 512)
        return _ffn_out(h, ffn_out, (layer, j), xf, g, b, 1024, FFN_DIM // 11, 0.5)

    for layer in range(DEPTH):
        xf, xb = ffn(xf, xb, layer, 0)

        g, b = ln_params(layer, 1)
        if layer % 2 == 0:
            w_in = even_in[layer // 2]
            w_a = w_in[:, :A_COLS].astype(BF16)
            w_i = jnp.pad(w_in[:, A_COLS:A_COLS + I_COLS], ((0, 0), (0, I_PAD - I_COLS))).astype(BF16)
            w_b = w_in[:, A_COLS + I_COLS:].astype(BF16)
            qkv_a = _proj_rope(xb, w_a, [tab_head], modes_a, seq, 512, False, "proj_dsa")[0]
            qk_i, w_idx = _proj_rope(xb, w_i, [tab_idx, tab_idx_key], modes_i, seq, 512, True, "proj_idx")
            qkv_b = _matmul(xb, w_b, 1024, 1024, "proj_band")
            o_a = _dsa_attention(qkv_a, qk_i, w_idx, batch, seq)
            o_b = _band_attention(qkv_b, _band_bias(even_rel_bias[layer // 2]), batch, seq)
            xf, xb = _out_ln([o_a, o_b], even_out[layer // 2].astype(BF16), xf, g, b, 512, "even_out_ln")
        else:
            qkv = _matmul(xb, odd_in, 1024, 1024, "proj_odd", lead=(layer // 2,))
            o = _stick_breaking(qkv, batch, seq, 256, 4)
            xf, xb = _out_ln([o], odd_out[layer // 2].astype(BF16), xf, g, b, 512, "odd_out_ln")

        g, b = ln_params(layer, 2)
        kv = _matmul(memb, xattn_kv[layer].astype(BF16), 1024, 1024, "proj_mem")
        xf, xb = _xattn(xb, xf, kv, xattn_q[layer].astype(BF16), xattn_o[layer].astype(BF16), g, b, seq, 512)

        xf, xb = ffn(xf, xb, layer, 1)

    return xf.reshape(batch, seq, d)
```

```python
import functools

import jax
import jax.numpy as jnp
from jax import lax
from jax.experimental import pallas as pl
from jax.experimental.pallas import tpu as pltpu

F32 = jnp.float32
BF16 = jnp.bfloat16

D_MODEL = 2048
DEPTH = 4
CHUNK = 64
MEM_LEN = 256
HEAD_DIM = 128
ROPE_THETA = 10000.0
LN_EPS = 1e-5
NEG_INF = -1e30
N_HEADS_A = 8
N_HEADS_B = 8
IDX_HEADS = 16
IDX_DIM = 64
TOPK_MAX = 256
LEFT_CHUNKS = 8
REL_CLIP = 128
REL_SIZE = CHUNK + REL_CLIP
N_HEADS_C = 16
N_HEADS_X = 4
FFN_DIM = ((8 * D_MODEL // 3 + 255) // 256) * 256
ALPHA = (2.0 * DEPTH) ** 0.25
SCALE = HEAD_DIM ** -0.5
LOG2_E = 1.4426950408889634

LANE = 128
INT_MIN = -2147483648
MIB = 1024 * 1024

A_COLS = (N_HEADS_A + 2) * HEAD_DIM
I_COLS = IDX_HEADS * IDX_DIM + IDX_DIM + IDX_HEADS
I_PAD = ((I_COLS + LANE - 1) // LANE) * LANE
W_IDX_LANE = IDX_DIM
B_COLS = 3 * N_HEADS_B * HEAD_DIM

BAND_Q = 2 * CHUNK
BAND_KB = (LEFT_CHUNKS * CHUNK) // BAND_Q + 1
BAND_W = BAND_KB * BAND_Q

LN_SUB_ROWS = 256

SB_DEAD_BITS = 152.0

DSA_KEY_SPAN = 512
DSA_HEAD_GROUP = 4
BIT_GROUP = 32 * 8
DSA_ROW_GROUP = 32


def _cparams(sem, vmem_mib):
    return pltpu.CompilerParams(dimension_semantics=sem, vmem_limit_bytes=vmem_mib * MIB)


def _dot(a, b):
    return jnp.dot(a, b, preferred_element_type=F32)


def _dot_nt(a, b):
    return lax.dot_general(a, b, (((1,), (1,)), ((), ())), preferred_element_type=F32)


def _layer_norm(y, g, b):
    mu = jnp.mean(y, axis=-1, keepdims=True)
    d = y - mu
    var = jnp.mean(d * d, axis=-1, keepdims=True)
    return d * lax.rsqrt(var + LN_EPS) * g + b


def _mm_kernel(x_ref, w_ref, o_ref, *scratch):
    if scratch:
        wb_ref, = scratch

        @pl.when(pl.program_id(1) == 0)
        def _():
            wb_ref[...] = w_ref[...].astype(BF16)
    else:
        wb_ref = w_ref
    o_ref[...] = _dot(x_ref[...], wb_ref[...]).astype(o_ref.dtype)


def _matmul(x, w, tm, tn, name, lead=()):
    m, k = x.shape
    n = w.shape[-1]
    cast = w.dtype != BF16
    return pl.pallas_call(
        _mm_kernel,
        grid=(n // tn, m // tm),
        in_specs=[pl.BlockSpec((tm, k), lambda j, i: (i, 0)),
                  pl.BlockSpec((None,) * len(lead) + (k, tn), lambda j, i: lead + (0, j))],
        out_specs=pl.BlockSpec((tm, tn), lambda j, i: (i, j)),
        out_shape=jax.ShapeDtypeStruct((m, n), BF16),
        scratch_shapes=[pltpu.VMEM((k, tn), BF16)] if cast else [],
        compiler_params=_cparams(("parallel", "arbitrary"), 48),
        name=name,
    )(x, w)


def _rope_group(x, cos, sin, width):
    if width == LANE:
        partner = pltpu.roll(x, LANE // 2, 1)
    else:
        lane = lax.broadcasted_iota(jnp.int32, x.shape, 1)
        half = width // 2
        partner = jnp.where((lane & half) == 0, pltpu.roll(x, LANE - half, 1), pltpu.roll(x, half, 1))
    return x * cos + partner * sin


def _proj_rope_kernel(x_ref, w_ref, *rest, modes, n_tab, f32_tail):
    tabs = rest[:2 * n_tab]
    outs = rest[2 * n_tab:]
    acc = _dot(x_ref[...], w_ref[...])
    for g, mode in enumerate(modes):
        blk = acc[:, g * LANE:(g + 1) * LANE]
        if mode is not None:
            tid, width = mode
            blk = _rope_group(blk, tabs[2 * tid][...], tabs[2 * tid + 1][...], width)
        outs[0][:, g * LANE:(g + 1) * LANE] = blk.astype(BF16)
    if f32_tail:
        outs[1][...] = acc[:, -LANE:]


def _proj_rope(x, w, tabs, modes, seq, tm, f32_tail, name):
    m, k = x.shape
    n = w.shape[1]
    per_seq = seq // tm
    in_specs = [pl.BlockSpec((tm, k), lambda i: (i, 0)), pl.BlockSpec((k, n), lambda i: (0, 0))]
    args = [x, w]
    for cos, sin in tabs:
        in_specs += [pl.BlockSpec((tm, LANE), lambda i: (i % per_seq, 0))] * 2
        args += [cos, sin]
    out_specs = [pl.BlockSpec((tm, n), lambda i: (i, 0))]
    out_shape = [jax.ShapeDtypeStruct((m, n), BF16)]
    if f32_tail:
        out_specs.append(pl.BlockSpec((tm, LANE), lambda i: (i, 0)))
        out_shape.append(jax.ShapeDtypeStruct((m, LANE), F32))
    return pl.pallas_call(
        functools.partial(_proj_rope_kernel, modes=modes, n_tab=len(tabs), f32_tail=f32_tail),
        grid=(m // tm,),
        in_specs=in_specs,
        out_specs=out_specs,
        out_shape=out_shape,
        compiler_params=_cparams(("parallel",), 48),
        name=name,
    )(*args)


def _ffn_in_kernel(x_ref, wa_ref, wg_ref, o_ref, wab_ref, wgb_ref):
    @pl.when(pl.program_id(1) == 0)
    def _():
        wab_ref[...] = wa_ref[...].astype(BF16)
        wgb_ref[...] = wg_ref[...].astype(BF16)

    x = x_ref[...]
    a = _dot(x, wab_ref[...])
    g = _dot(x, wgb_ref[...])
    o_ref[...] = (a * jax.nn.sigmoid(a) * g).astype(o_ref.dtype)


def _ffn_in(xb, w, lead, tm, tf):
    m, k = xb.shape
    nf = FFN_DIM // tf
    wblock = (None,) * len(lead) + (k, tf)
    return pl.pallas_call(
        _ffn_in_kernel,
        grid=(nf, m // tm),
        in_specs=[pl.BlockSpec((tm, k), lambda j, i: (i, 0)),
                  pl.BlockSpec(wblock, lambda j, i: lead + (0, j)),
                  pl.BlockSpec(wblock, lambda j, i: lead + (0, j + nf))],
        out_specs=pl.BlockSpec((tm, tf), lambda j, i: (i, j)),
        out_shape=jax.ShapeDtypeStruct((m, FFN_DIM), BF16),
        scratch_shapes=[pltpu.VMEM((k, tf), BF16), pltpu.VMEM((k, tf), BF16)],
        compiler_params=_cparams(("parallel", "arbitrary"), 56),
        name="ffn_in",
    )(xb, w, w)


def _ffn_out_kernel(h_ref, w_ref, x_ref, g_ref, b_ref, of_ref, ob_ref, *, nk, scale):
    k = pl.program_id(1)

    @pl.when(k == 0)
    def _():
        of_ref[...] = _dot(h_ref[...], w_ref[...].astype(BF16))

    @pl.when(jnp.logical_and(k > 0, k < nk - 1))
    def _():
        of_ref[...] += _dot(h_ref[...], w_ref[...].astype(BF16))

    @pl.when(k == nk - 1)
    def _():
        w = w_ref[...].astype(BF16)
        for r in range(0, of_ref.shape[0], LN_SUB_ROWS):
            rows = slice(r, r + LN_SUB_ROWS)
            acc = of_ref[rows, :] + _dot(h_ref[rows, :], w)
            y = _layer_norm(ALPHA * x_ref[rows, :] + scale * acc, g_ref[...], b_ref[...])
            of_ref[rows, :] = y
            ob_ref[rows, :] = y.astype(BF16)


def _ffn_out(h, w, lead, xf, g, b, tm, tk, scale):
    m, kdim = h.shape
    n = w.shape[-1]
    nk = kdim // tk
    row = lambda i, k: (i, 0)
    return pl.pallas_call(
        functools.partial(_ffn_out_kernel, nk=nk, scale=scale),
        grid=(m // tm, nk),
        in_specs=[pl.BlockSpec((tm, tk), lambda i, k: (i, k)),
                  pl.BlockSpec((None,) * len(lead) + (tk, n), lambda i, k: lead + (k, 0)),
                  pl.BlockSpec((tm, n), row),
                  pl.BlockSpec((1, n), lambda i, k: (0, 0)),
                  pl.BlockSpec((1, n), lambda i, k: (0, 0))],
        out_specs=[pl.BlockSpec((tm, n), row), pl.BlockSpec((tm, n), row)],
        out_shape=[jax.ShapeDtypeStruct((m, n), F32), jax.ShapeDtypeStruct((m, n), BF16)],
        compiler_params=_cparams(("parallel", "arbitrary"), 60),
        name="ffn_out_ln",
    )(h, w, xf, g, b)


def _out_ln_kernel(*refs, n_lhs):
    lhs = refs[:n_lhs]
    ws = refs[n_lhs:2 * n_lhs]
    x_ref, g_ref, b_ref, of_ref, ob_ref = refs[2 * n_lhs:]
    for r in range(0, of_ref.shape[0], LN_SUB_ROWS):
        rows = slice(r, r + LN_SUB_ROWS)
        acc = _dot(lhs[0][rows, :], ws[0][...])
        for l_ref, w_ref in zip(lhs[1:], ws[1:]):
            acc = acc + _dot(l_ref[rows, :], w_ref[...])
        y = _layer_norm(ALPHA * x_ref[rows, :] + acc, g_ref[...], b_ref[...])
        of_ref[rows, :] = y
        ob_ref[rows, :] = y.astype(BF16)


def _out_ln(lhs_list, w, xf, g, b, tm, name):
    m, n = xf.shape
    row = lambda i: (i, 0)
    in_specs, args, off = [], [], 0
    for l in lhs_list:
        in_specs.append(pl.BlockSpec((tm, l.shape[1]), row))
        args.append(l)
    for l in lhs_list:
        kl = l.shape[1]
        in_specs.append(pl.BlockSpec((kl, n), functools.partial(lambda i, o: (o, 0), o=off // kl)))
        args.append(w)
        off += kl
    in_specs += [pl.BlockSpec((tm, n), row), pl.BlockSpec((1, n), lambda i: (0, 0)), pl.BlockSpec((1, n), lambda i: (0, 0))]
    args += [xf, g, b]
    return pl.pallas_call(
        functools.partial(_out_ln_kernel, n_lhs=len(lhs_list)),
        grid=(m // tm,),
        in_specs=in_specs,
        out_specs=[pl.BlockSpec((tm, n), row), pl.BlockSpec((tm, n), row)],
        out_shape=[jax.ShapeDtypeStruct((m, n), F32), jax.ShapeDtypeStruct((m, n), BF16)],
        compiler_params=_cparams(("parallel",), 56),
        name=name,
    )(*args)


def _dsa_kernel(q_ref, k_ref, v_ref, qi_ref, ki_ref, w_ref, o_ref, plane_ref, bias_ref, *, tq, seq, topk):
    i = pl.program_id(1)
    per_span = DSA_KEY_SPAN // tq
    for n in range(1, seq // DSA_KEY_SPAN + 1):
        pl.when(i // per_span == n - 1)(functools.partial(
            _dsa_body, i, q_ref, k_ref, v_ref, qi_ref, ki_ref, w_ref, o_ref, plane_ref, bias_ref,
            tq=tq, nk=n * DSA_KEY_SPAN, topk=topk))


def _dsa_body(i, q_ref, k_ref, v_ref, qi_ref, ki_ref, w_ref, o_ref, plane_ref, bias_ref, *, tq, nk, topk):
    span = DSA_KEY_SPAN
    n_groups = nk // BIT_GROUP
    w_t = w_ref[...].T
    key_pos = nk - span + lax.broadcasted_iota(jnp.int32, (span, tq), 0)
    q_pos = i * tq + lax.broadcasted_iota(jnp.int32, (span, tq), 1)
    allowed = (key_pos >> 6) <= (q_pos >> 6)
    qi_all = jnp.concatenate([qi_ref[:, h * IDX_DIM:(h + 1) * IDX_DIM] for h in range(IDX_HEADS)], axis=0)
    for c in range(nk // span):
        rel = jnp.maximum(_dot_nt(ki_ref[c * span:(c + 1) * span, :IDX_DIM], qi_all), 0.0)
        score = jnp.zeros((span, tq), F32)
        for h in range(IDX_HEADS):
            score = score + rel[:, h * tq:(h + 1) * tq] * w_t[W_IDX_LANE + h:W_IDX_LANE + h + 1, :]
        if c == nk // span - 1:
            score = jnp.where(allowed, score, NEG_INF)
        bits = lax.bitcast_convert_type(score, jnp.int32)
        u = bits ^ ((bits >> 31) | INT_MIN)
        for gg in range(span // BIT_GROUP):
            words = [u[gg * BIT_GROUP + 8 * j:gg * BIT_GROUP + 8 * j + 8, :] for j in range(32)]
            g = c * (span // BIT_GROUP) + gg
            for b, plane in enumerate(_bit_transpose32(words)):
                plane_ref[g, b] = plane

    def select_bit(j, state):
        alive, keep, above = state[:n_groups], state[n_groups:2 * n_groups], state[-1]
        ones = [a & plane_ref[g, 31 - j] for g, a in enumerate(alive)]
        cnt = lax.population_count(ones[0])
        for o in ones[1:]:
            cnt = cnt + lax.population_count(o)
        for shift in (4, 2, 1):
            cnt = cnt + pltpu.roll(cnt, shift, 0)
        take = (above + cnt) >= topk
        zeros = [a ^ o for a, o in zip(alive, ones)]
        new_alive = [jnp.where(take, o, z) for o, z in zip(ones, zeros)]
        new_keep = [jnp.where(take, kp ^ z, kp) for kp, z in zip(keep, zeros)]
        return tuple(new_alive) + tuple(new_keep) + (jnp.where(take, above, above + cnt),)

    everything = jnp.full((8, tq), -1, jnp.int32)
    state = lax.fori_loop(0, 32, select_bit, (everything,) * (2 * n_groups) + (jnp.zeros((8, tq), jnp.int32),))
    alive, keep, above = state[:n_groups], state[n_groups:2 * n_groups], state[-1]

    need = topk - above
    sub = lax.broadcasted_iota(jnp.int32, (8, tq), 0)

    def below(limit, g):
        n_bits = jnp.clip((limit - g * BIT_GROUP - sub + 7) >> 3, 0, 32)
        return jnp.where(n_bits >= 32, -1, lax.shift_left(jnp.int32(1), jnp.minimum(n_bits, 31)) - 1)

    bound = jnp.zeros((8, tq), jnp.int32)
    step = 1 << ((nk - 1).bit_length() - 1)
    while step >= 1:
        cnt = lax.population_count(alive[0] & below(bound + step, 0))
        for g in range(1, n_groups):
            cnt = cnt + lax.population_count(alive[g] & below(bound + step, g))
        for shift in (4, 2, 1):
            cnt = cnt + pltpu.roll(cnt, shift, 0)
        bound = jnp.where(cnt < need, bound + step, bound)
        step //= 2
    keep = [kp ^ (a & ~below(bound + 1, g)) for g, (kp, a) in enumerate(zip(keep, alive))]

    for g in range(n_groups):
        for j in range(32):
            rows = slice(g * BIT_GROUP + 8 * j, g * BIT_GROUP + 8 * j + 8)
            bias = jnp.where(lax.shift_left(keep[g], 31 - j) < 0, 0.0, NEG_INF)
            if g * BIT_GROUP + 8 * j >= nk - span:
                bias = jnp.where(allowed[rows.start - (nk - span):rows.stop - (nk - span), :], bias, NEG_INF)
            bias_ref[rows, :] = bias

    k = k_ref[:nk, :]
    v_ext = jnp.concatenate([v_ref[:nk, :], jnp.ones((nk, HEAD_DIM), BF16)], axis=1)
    groups = [[slice(h * HEAD_DIM, (h + 1) * HEAD_DIM) for h in range(g * DSA_HEAD_GROUP, (g + 1) * DSA_HEAD_GROUP)]
              for g in range(N_HEADS_A // DSA_HEAD_GROUP)]
    logits = [_dot_nt(k, jnp.concatenate([q_ref[:, cols] for cols in heads], axis=0)) * SCALE for heads in groups]
    probs = []
    for lg in logits:
        lg = jnp.concatenate([lg[:, j * tq:(j + 1) * tq] + bias_ref[:nk, :] for j in range(DSA_HEAD_GROUP)], axis=1)
        probs.append(jnp.exp(lg - jnp.max(lg, axis=0, keepdims=True)).astype(BF16))
    outs = [lax.dot_general(e, v_ext, (((0,), (0,)), ((), ())), preferred_element_type=F32) for e in probs]
    for heads, o in zip(groups, outs):
        o = o[:, :HEAD_DIM] / o[:, HEAD_DIM:HEAD_DIM + 1]
        for j, cols in enumerate(heads):
            o_ref[:, cols] = o[j * tq:(j + 1) * tq].astype(BF16)


def _bit_transpose32(words):
    words = list(words)
    for dist, low in ((16, 0x0000FFFF), (8, 0x00FF00FF), (4, 0x0F0F0F0F), (2, 0x33333333), (1, 0x55555555)):
        for k in range(32):
            if k & dist:
                continue
            a, b = words[k], words[k + dist]
            t = (lax.shift_right_logical(a, dist) ^ b) & low
            words[k + dist] = b ^ t
            words[k] = a ^ lax.shift_left(t, dist)
    return words


def _dsa_attention(qkv_a, qk_i, w_i, batch, seq):
    tq = 2 * CHUNK
    nq = seq // tq
    hq = N_HEADS_A * HEAD_DIM
    kcol = hq // LANE
    icol = (IDX_HEADS * IDX_DIM) // LANE
    qmap = lambda b, i: (b * nq + i, 0)
    return pl.pallas_call(
        functools.partial(_dsa_kernel, tq=tq, seq=seq, topk=min(TOPK_MAX, seq // 4)),
        grid=(batch, nq),
        in_specs=[pl.BlockSpec((tq, hq), qmap),
                  pl.BlockSpec((seq, LANE), lambda b, i: (b, kcol)),
                  pl.BlockSpec((seq, LANE), lambda b, i: (b, kcol + 1)),
                  pl.BlockSpec((tq, IDX_HEADS * IDX_DIM), qmap),
                  pl.BlockSpec((seq, LANE), lambda b, i: (b, icol)),
                  pl.BlockSpec((tq, LANE), qmap)],
        out_specs=pl.BlockSpec((tq, hq), qmap),
        out_shape=jax.ShapeDtypeStruct((batch * seq, hq), BF16),
        scratch_shapes=[pltpu.VMEM((seq // BIT_GROUP, 32, 8, tq), jnp.int32), pltpu.VMEM((seq, tq), F32)],
        compiler_params=_cparams(("parallel", "parallel"), 48),
        name="dsa_attention",
    )(qkv_a, qkv_a, qkv_a, qk_i, qk_i, w_i)


def _band_bias_kernel(rel_ref, o_ref):
    h = pl.program_id(0)
    i = lax.broadcasted_iota(jnp.int32, (BAND_Q, BAND_Q), 0)
    jj = lax.broadcasted_iota(jnp.int32, (BAND_Q, BAND_Q), 1)
    for jb in range(BAND_KB):
        j = jb * BAND_Q + jj
        idx = jnp.clip(i - j + LEFT_CHUNKS * CHUNK, -(CHUNK - 1), REL_CLIP) + (CHUNK - 1)
        cq = i >> 6
        ck = j >> 6
        d_lo = LEFT_CHUNKS * CHUNK - (jb + 1) * BAND_Q + 1
        d_hi = LEFT_CHUNKS * CHUNK - jb * BAND_Q + BAND_Q - 1
        r_lo = min(max(d_lo, -(CHUNK - 1)), REL_CLIP) + (CHUNK - 1)
        r_hi = min(max(d_hi, -(CHUNK - 1)), REL_CLIP) + (CHUNK - 1)
        val = lax.fori_loop(r_lo, r_hi + 1, lambda r, acc: jnp.where(idx == r, rel_ref[h, r], acc),
                            jnp.zeros((BAND_Q, BAND_Q), F32))
        val = jnp.where(ck >= cq, val, NEG_INF)
        o_ref[0, jb] = jnp.where(ck <= cq + LEFT_CHUNKS, val, NEG_INF)


def _band_bias(rel_bias):
    nh = rel_bias.shape[0]
    return pl.pallas_call(
        _band_bias_kernel,
        grid=(nh,),
        in_specs=[pl.BlockSpec(memory_space=pltpu.SMEM)],
        out_specs=pl.BlockSpec((1, BAND_KB, BAND_Q, BAND_Q), lambda h: (h, 0, 0, 0)),
        out_shape=jax.ShapeDtypeStruct((nh, BAND_KB, BAND_Q, BAND_Q), F32),
        name="band_bias",
    )(rel_bias)


def _band_kernel(q_ref, k_ref, v_ref, bias_ref, o_ref):
    p = pl.program_id(1)
    first = BAND_KB - 1
    shift = jnp.maximum(first - p, 0)
    s0 = pl.multiple_of(jnp.maximum(p - first, 0) * BAND_Q, BAND_Q)
    heads = [slice(h * HEAD_DIM, (h + 1) * HEAD_DIM) for h in range(N_HEADS_B)]
    logits = [_dot_nt(q_ref[:, cols], k_ref[pl.ds(s0, BAND_W), cols]) * SCALE for cols in heads]
    probs, sums = [], []
    for h, lg in enumerate(logits):
        blocks = []
        for jb in range(BAND_KB):
            src = jb + shift
            blk = lg[:, jb * BAND_Q:(jb + 1) * BAND_Q] + bias_ref[h, jnp.minimum(src, first)]
            blocks.append(jnp.where(src <= first, blk, NEG_INF))
        lg = jnp.concatenate(blocks, axis=1)
        e = jnp.exp(lg - jnp.max(lg, axis=1, keepdims=True))
        sums.append(jnp.sum(e, axis=1, keepdims=True))
        probs.append(e.astype(BF16))
    outs = [_dot(e, v_ref[pl.ds(s0, BAND_W), cols]) for e, cols in zip(probs, heads)]
    for o, s, cols in zip(outs, sums, heads):
        o_ref[:, cols] = (o / s).astype(BF16)


def _band_attention(qkv_b, bias, batch, seq):
    nq = seq // BAND_Q
    hq = N_HEADS_B * HEAD_DIM
    qmap = lambda b, p: (b * nq + p, 0)
    return pl.pallas_call(
        _band_kernel,
        grid=(batch, nq),
        in_specs=[pl.BlockSpec((BAND_Q, hq), qmap),
                  pl.BlockSpec((seq, hq), lambda b, p: (b, 1)),
                  pl.BlockSpec((seq, hq), lambda b, p: (b, 2)),
                  pl.BlockSpec(bias.shape, lambda b, p: (0, 0, 0, 0))],
        out_specs=pl.BlockSpec((BAND_Q, hq), qmap),
        out_shape=jax.ShapeDtypeStruct((batch * seq, hq), BF16),
        compiler_params=_cparams(("parallel", "parallel"), 48),
        name="band_attention",
    )(qkv_b, qkv_b, qkv_b, bias)


def _sb_blocks(qs, k_blks, v_blks, later, carry, past):
    zs = [_dot_nt(q, k) * (SCALE * LOG2_E) for q, k in zip(qs, k_blks)]
    log_betas, drops, splits = [], [], []
    for z2 in zs:
        neg_abs = lax.bitcast_convert_type(lax.bitcast_convert_type(z2, jnp.int32) | INT_MIN, F32)
        drop = jnp.maximum(z2, 0.0) + jnp.log2(1.0 + jnp.exp2(neg_abs))
        log_betas.append(z2 - drop)
        if past is not None:
            drop = jnp.where(past, drop, 0.0)
        drops.append(drop)
        hi = drop.astype(BF16)
        r1 = drop - hi.astype(F32)
        mid = r1.astype(BF16)
        lo = (r1 - mid.astype(F32)).astype(BF16)
        splits.append(jnp.concatenate([hi, mid, lo], axis=1))
    afters = [_dot(s, later) for s in splits]
    probs, tails = [], []
    for h, (log_beta, drop, after) in enumerate(zip(log_betas, drops, afters)):
        row_sum = jnp.sum(drop, axis=1, keepdims=True)
        if carry is None:
            a = jnp.exp2(log_beta - after)
            tails.append(row_sum)
        else:
            a = jnp.exp2(log_beta - carry[2 * h] - after)
            tails.append(carry[2 * h] + row_sum)
        if past is not None:
            a = jnp.where(past, a, 0.0)
        probs.append(a.astype(BF16))
    pvs = [_dot(a, v) for a, v in zip(probs, v_blks)]
    out = []
    for h, (tail, pv) in enumerate(zip(tails, pvs)):
        out += [tail, pv if carry is None else carry[2 * h + 1] + pv]
    return tuple(out)


def _sb_kernel(q_ref, k_ref, v_ref, o_ref, *, tq, nh):
    i = pl.program_id(2)
    r3 = lax.broadcasted_iota(jnp.int32, (3 * tq, tq), 0)
    c3 = lax.broadcasted_iota(jnp.int32, (3 * tq, tq), 1)
    later = jnp.where((r3 & (tq - 1)) > c3, 1.0, 0.0).astype(BF16)
    row = lax.broadcasted_iota(jnp.int32, (tq, tq), 0)
    col = lax.broadcasted_iota(jnp.int32, (tq, tq), 1)
    heads = [slice(h * HEAD_DIM, (h + 1) * HEAD_DIM) for h in range(nh)]

    def walk(s0, carry, past):
        return _sb_blocks([q_ref[:, cols] for cols in heads],
                          [k_ref[pl.ds(s0, tq), cols] for cols in heads],
                          [v_ref[pl.ds(s0, tq), cols] for cols in heads], later, carry, past)

    def least_tail(c):
        m = c[0]
        for h in range(1, nh):
            m = jnp.minimum(m, c[2 * h])
        return jnp.min(m)

    def alive(state):
        return jnp.logical_and(state[0] < i, state[1] < SB_DEAD_BITS)

    def step(state):
        c = walk(pl.multiple_of((i - 1 - state[0]) * tq, tq), state[2:], None)
        return (state[0] + 1, least_tail(c)) + c

    carry = walk(pl.multiple_of(i * tq, tq), None, col < row)
    carry = lax.while_loop(alive, step, (jnp.int32(0), least_tail(carry)) + carry)[2:]
    for h, cols in enumerate(heads):
        o_ref[:, cols] = carry[2 * h + 1].astype(BF16)


def _stick_breaking(qkv, batch, seq, tq, nh):
    nq = seq // tq
    ng = N_HEADS_C // nh
    w = nh * HEAD_DIM
    qmap = lambda b, g, i: (b * nq + i, g)
    return pl.pallas_call(
        functools.partial(_sb_kernel, tq=tq, nh=nh),
        grid=(batch, ng, nq),
        in_specs=[pl.BlockSpec((tq, w), qmap),
                  pl.BlockSpec((seq, w), lambda b, g, i: (b, ng + g)),
                  pl.BlockSpec((seq, w), lambda b, g, i: (b, 2 * ng + g))],
        out_specs=pl.BlockSpec((tq, w), qmap),
        out_shape=jax.ShapeDtypeStruct((batch * seq, N_HEADS_C * HEAD_DIM), BF16),
        compiler_params=_cparams(("parallel", "parallel", "parallel"), 32),
        name="stick_breaking",
    )(qkv, qkv, qkv)


def _xattn_kernel(xb_ref, xf_ref, kv_ref, wq_ref, wo_ref, g_ref, b_ref, of_ref, ob_ref):
    hk = N_HEADS_X * HEAD_DIM
    heads = [slice(h * HEAD_DIM, (h + 1) * HEAD_DIM) for h in range(N_HEADS_X)]
    for r in range(0, of_ref.shape[0], LN_SUB_ROWS):
        rows = slice(r, r + LN_SUB_ROWS)
        q = _dot(xb_ref[rows, :], wq_ref[...]).astype(BF16)
        logits = [_dot_nt(q[:, cols], kv_ref[:, cols]) * SCALE for cols in heads]
        probs, sums = [], []
        for lg in logits:
            e = jnp.exp(lg - jnp.max(lg, axis=1, keepdims=True))
            sums.append(jnp.sum(e, axis=1, keepdims=True))
            probs.append(e.astype(BF16))
        outs = [_dot(e, kv_ref[:, hk + h * HEAD_DIM:hk + (h + 1) * HEAD_DIM]) for h, e in enumerate(probs)]
        o = jnp.concatenate([(oh / s).astype(BF16) for oh, s in zip(outs, sums)], axis=1)
        y = _layer_norm(ALPHA * xf_ref[rows, :] + _dot(o, wo_ref[...]), g_ref[...], b_ref[...])
        of_ref[rows, :] = y
        ob_ref[rows, :] = y.astype(BF16)


def _xattn(xb, xf, kv, wq, wo, g, b, seq, tm):
    m, n = xf.shape
    per_seq = seq // tm
    row = lambda i: (i, 0)
    const = lambda i: (0, 0)
    return pl.pallas_call(
        _xattn_kernel,
        grid=(m // tm,),
        in_specs=[pl.BlockSpec((tm, n), row), pl.BlockSpec((tm, n), row),
                  pl.BlockSpec((MEM_LEN, kv.shape[1]), lambda i: (i // per_seq, 0)),
                  pl.BlockSpec(wq.shape, const), pl.BlockSpec(wo.shape, const),
                  pl.BlockSpec((1, n), const), pl.BlockSpec((1, n), const)],
        out_specs=[pl.BlockSpec((tm, n), row), pl.BlockSpec((tm, n), row)],
        out_shape=[jax.ShapeDtypeStruct((m, n), F32), jax.ShapeDtypeStruct((m, n), BF16)],
        compiler_params=_cparams(("parallel",), 56),
        name="xattn_ln",
    )(xb, xf, kv, wq, wo, g, b)


def _rope_tables(seq):
    pos = jnp.arange(seq).astype(F32)[:, None]

    def half_tables(half):
        inv_freq = ROPE_THETA ** (-jnp.arange(half, dtype=F32) / half)
        ang = pos * inv_freq[None, :]
        return jnp.cos(ang), jnp.sin(ang)

    c64, s64 = half_tables(HEAD_DIM // 2)
    c32, s32 = half_tables(IDX_DIM // 2)
    head = (jnp.concatenate([c64, c64], 1), jnp.concatenate([-s64, s64], 1))
    idx = (jnp.concatenate([c32, c32, c32, c32], 1), jnp.concatenate([-s32, s32, -s32, s32], 1))
    rest = LANE - IDX_DIM
    idx_key = (jnp.concatenate([c32, c32, jnp.ones((seq, rest), F32)], 1),
               jnp.concatenate([-s32, s32, jnp.zeros((seq, rest), F32)], 1))
    return head, idx, idx_key


def kernel(x, mem, ln_g, ln_b, ffn_in, ffn_out, xattn_q, xattn_kv, xattn_o,
           even_in, even_out, even_rel_bias, odd_in, odd_out):
    batch, seq, d = x.shape
    t = batch * seq
    xf = x.reshape(t, d)
    xb = xf.astype(BF16)
    memb = mem.reshape(batch * MEM_LEN, d).astype(BF16)
    tab_head, tab_idx, tab_idx_key = _rope_tables(seq)

    modes_a = ((0, HEAD_DIM),) * (N_HEADS_A + 1) + (None,)
    modes_i = ((0, IDX_DIM),) * ((IDX_HEADS * IDX_DIM) // LANE) + ((1, IDX_DIM),)

    def ln_params(layer, j):
        return ln_g[layer, j][None, :], ln_b[layer, j][None, :]

    def ffn(xf, xb, layer, j):
        g, b = ln_params(layer, 3 * j)
        h = _ffn_in(xb, ffn_in, (layer, j), 2048, 512)
        return _ffn_out(h, ffn_out, (layer, j), xf, g, b, 1024, FFN_DIM // 11, 0.5)

    for layer in range(DEPTH):
        xf, xb = ffn(xf, xb, layer, 0)

        g, b = ln_params(layer, 1)
        if layer % 2 == 0:
            w_in = even_in[layer // 2]
            w_a = w_in[:, :A_COLS].astype(BF16)
            w_i = jnp.pad(w_in[:, A_COLS:A_COLS + I_COLS], ((0, 0), (0, I_PAD - I_COLS))).astype(BF16)
            w_b = w_in[:, A_COLS + I_COLS:].astype(BF16)
            qkv_a = _proj_rope(xb, w_a, [tab_head], modes_a, seq, 512, False, "proj_dsa")[0]
            qk_i, w_idx = _proj_rope(xb, w_i, [tab_idx, tab_idx_key], modes_i, seq, 512, True, "proj_idx")
            qkv_b = _matmul(xb, w_b, 2048, 1024, "proj_band")
            o_a = _dsa_attention(qkv_a, qk_i, w_idx, batch, seq)
            o_b = _band_attention(qkv_b, _band_bias(even_rel_bias[layer // 2]), batch, seq)
            xf, xb = _out_ln([o_a, o_b], even_out[layer // 2].astype(BF16), xf, g, b, 512, "even_out_ln")
        else:
            qkv = _matmul(xb, odd_in, 1024, 1024, "proj_odd", lead=(layer // 2,))
            o = _stick_breaking(qkv, batch, seq, 256, 4)
            xf, xb = _out_ln([o], odd_out[layer // 2].astype(BF16), xf, g, b, 512, "odd_out_ln")

        g, b = ln_params(layer, 2)
        kv = _matmul(memb, xattn_kv[layer].astype(BF16), 1024, 1024, "proj_mem")
        xf, xb = _xattn(xb, xf, kv, xattn_q[layer].astype(BF16), xattn_o[layer].astype(BF16), g, b, seq, 512)

        xf, xb = ffn(xf, xb, layer, 1)

    return xf.reshape(batch, seq, d)
```

```python
import functools

import jax
import jax.numpy as jnp
from jax import lax
from jax.experimental import pallas as pl
from jax.experimental.pallas import tpu as pltpu

F32 = jnp.float32
BF16 = jnp.bfloat16

D_MODEL = 2048
DEPTH = 4
CHUNK = 64
MEM_LEN = 256
HEAD_DIM = 128
ROPE_THETA = 10000.0
LN_EPS = 1e-5
NEG_INF = -1e30
N_HEADS_A = 8
N_HEADS_B = 8
IDX_HEADS = 16
IDX_DIM = 64
TOPK_MAX = 256
LEFT_CHUNKS = 8
REL_CLIP = 128
REL_SIZE = CHUNK + REL_CLIP
N_HEADS_C = 16
N_HEADS_X = 4
FFN_DIM = ((8 * D_MODEL // 3 + 255) // 256) * 256
ALPHA = (2.0 * DEPTH) ** 0.25
SCALE = HEAD_DIM ** -0.5
LOG2_E = 1.4426950408889634

LANE = 128
INT_MIN = -2147483648
MIB = 1024 * 1024

A_COLS = (N_HEADS_A + 2) * HEAD_DIM
I_COLS = IDX_HEADS * IDX_DIM + IDX_DIM + IDX_HEADS
I_PAD = ((I_COLS + LANE - 1) // LANE) * LANE
W_IDX_LANE = IDX_DIM
B_COLS = 3 * N_HEADS_B * HEAD_DIM

BAND_Q = 2 * CHUNK
BAND_KB = (LEFT_CHUNKS * CHUNK) // BAND_Q + 1
BAND_W = BAND_KB * BAND_Q

LN_SUB_ROWS = 256

SB_DEAD_BITS = 152.0

DSA_KEY_SPAN = 512
DSA_HEAD_GROUP = 4
BIT_GROUP = 32 * 8
DSA_ROW_GROUP = 32


def _cparams(sem, vmem_mib):
    return pltpu.CompilerParams(dimension_semantics=sem, vmem_limit_bytes=vmem_mib * MIB)


def _dot(a, b):
    return jnp.dot(a, b, preferred_element_type=F32)


def _dot_nt(a, b):
    return lax.dot_general(a, b, (((1,), (1,)), ((), ())), preferred_element_type=F32)


def _layer_norm(y, g, b):
    mu = jnp.mean(y, axis=-1, keepdims=True)
    d = y - mu
    var = jnp.mean(d * d, axis=-1, keepdims=True)
    return d * lax.rsqrt(var + LN_EPS) * g + b


def _mm_kernel(x_ref, w_ref, o_ref, *scratch):
    if scratch:
        wb_ref, = scratch

        @pl.when(pl.program_id(1) == 0)
        def _():
            wb_ref[...] = w_ref[...].astype(BF16)
    else:
        wb_ref = w_ref
    o_ref[...] = _dot(x_ref[...], wb_ref[...]).astype(o_ref.dtype)


def _matmul(x, w, tm, tn, name, lead=()):
    m, k = x.shape
    n = w.shape[-1]
    cast = w.dtype != BF16
    return pl.pallas_call(
        _mm_kernel,
        grid=(n // tn, m // tm),
        in_specs=[pl.BlockSpec((tm, k), lambda j, i: (i, 0)),
                  pl.BlockSpec((None,) * len(lead) + (k, tn), lambda j, i: lead + (0, j))],
        out_specs=pl.BlockSpec((tm, tn), lambda j, i: (i, j)),
        out_shape=jax.ShapeDtypeStruct((m, n), BF16),
        scratch_shapes=[pltpu.VMEM((k, tn), BF16)] if cast else [],
        compiler_params=_cparams(("parallel", "arbitrary"), 48),
        name=name,
    )(x, w)


def _rope_group(x, cos, sin, width):
    if width == LANE:
        partner = pltpu.roll(x, LANE // 2, 1)
    else:
        lane = lax.broadcasted_iota(jnp.int32, x.shape, 1)
        half = width // 2
        partner = jnp.where((lane & half) == 0, pltpu.roll(x, LANE - half, 1), pltpu.roll(x, half, 1))
    return x * cos + partner * sin


def _proj_rope_kernel(x_ref, w_ref, *rest, modes, n_tab, f32_tail):
    tabs = rest[:2 * n_tab]
    outs = rest[2 * n_tab:]
    acc = _dot(x_ref[...], w_ref[...])
    for g, mode in enumerate(modes):
        blk = acc[:, g * LANE:(g + 1) * LANE]
        if mode is not None:
            tid, width = mode
            blk = _rope_group(blk, tabs[2 * tid][...], tabs[2 * tid + 1][...], width)
        outs[0][:, g * LANE:(g + 1) * LANE] = blk.astype(BF16)
    if f32_tail:
        outs[1][...] = acc[:, -LANE:]


def _proj_rope(x, w, tabs, modes, seq, tm, f32_tail, name):
    m, k = x.shape
    n = w.shape[1]
    per_seq = seq // tm
    in_specs = [pl.BlockSpec((tm, k), lambda i: (i, 0)), pl.BlockSpec((k, n), lambda i: (0, 0))]
    args = [x, w]
    for cos, sin in tabs:
        in_specs += [pl.BlockSpec((tm, LANE), lambda i: (i % per_seq, 0))] * 2
        args += [cos, sin]
    out_specs = [pl.BlockSpec((tm, n), lambda i: (i, 0))]
    out_shape = [jax.ShapeDtypeStruct((m, n), BF16)]
    if f32_tail:
        out_specs.append(pl.BlockSpec((tm, LANE), lambda i: (i, 0)))
        out_shape.append(jax.ShapeDtypeStruct((m, LANE), F32))
    return pl.pallas_call(
        functools.partial(_proj_rope_kernel, modes=modes, n_tab=len(tabs), f32_tail=f32_tail),
        grid=(m // tm,),
        in_specs=in_specs,
        out_specs=out_specs,
        out_shape=out_shape,
        compiler_params=_cparams(("parallel",), 48),
        name=name,
    )(*args)


def _ffn_kernel(x_ref, wa_ref, wg_ref, wo_ref, g_ref, b_ref, *rest, nf, tf, scale, want_bf16):
    if want_bf16:
        of_ref, ob_ref, xb_ref = rest
    else:
        of_ref, xb_ref = rest
    j = pl.program_id(1)

    @pl.when(j == 0)
    def _():
        xb_ref[...] = x_ref[...].astype(BF16)

    subs = [slice(r, r + LN_SUB_ROWS) for r in range(0, of_ref.shape[0], LN_SUB_ROWS)]

    def hidden_slice(phase):
        w_in = jnp.concatenate([wa_ref[...], wg_ref[...]], axis=1).astype(BF16)
        wo = wo_ref[...].astype(BF16)
        ags = [_dot(xb_ref[rows, :], w_in) for rows in subs]
        hs = [(ag[:, :tf] * jax.nn.sigmoid(ag[:, :tf]) * ag[:, tf:]).astype(BF16) for ag in ags]
        parts = [_dot(h, wo) for h in hs]
        for rows, part in zip(subs, parts):
            if phase == "first":
                of_ref[rows, :] = part
            elif phase == "middle":
                of_ref[rows, :] += part
            else:
                y = _layer_norm(ALPHA * x_ref[rows, :] + scale * (of_ref[rows, :] + part), g_ref[...], b_ref[...])
                of_ref[rows, :] = y
                if want_bf16:
                    ob_ref[rows, :] = y.astype(BF16)

    pl.when(j == 0)(functools.partial(hidden_slice, "first"))
    pl.when(jnp.logical_and(j > 0, j < nf - 1))(functools.partial(hidden_slice, "middle"))
    pl.when(j == nf - 1)(functools.partial(hidden_slice, "last"))


def _ffn(xf, w_in, w_out, lead, g, b, tm, tf, scale, want_bf16):
    m, d = xf.shape
    nf = FFN_DIM // tf
    nl = (None,) * len(lead)
    row = lambda i, j: (i, 0)
    const = lambda i, j: (0, 0)
    out_specs = [pl.BlockSpec((tm, d), row)]
    out_shape = [jax.ShapeDtypeStruct((m, d), F32)]
    if want_bf16:
        out_specs.append(pl.BlockSpec((tm, d), row))
        out_shape.append(jax.ShapeDtypeStruct((m, d), BF16))
    return pl.pallas_call(
        functools.partial(_ffn_kernel, nf=nf, tf=tf, scale=scale, want_bf16=want_bf16),
        grid=(m // tm, nf),
        in_specs=[pl.BlockSpec((tm, d), row, pipeline_mode=pl.Buffered(1 if want_bf16 else 2)),
                  pl.BlockSpec(nl + (d, tf), lambda i, j: lead + (0, j)),
                  pl.BlockSpec(nl + (d, tf), lambda i, j: lead + (0, j + nf)),
                  pl.BlockSpec(nl + (tf, d), lambda i, j: lead + (j, 0)),
                  pl.BlockSpec((1, d), const), pl.BlockSpec((1, d), const)],
        out_specs=out_specs,
        out_shape=out_shape,
        scratch_shapes=[pltpu.VMEM((tm, d), BF16)],
        compiler_params=_cparams(("parallel", "arbitrary"), 60),
        name="ffn_fused",
    )(xf, w_in, w_in, w_out, g, b)


def _out_ln_kernel(*refs, n_lhs):
    lhs = refs[:n_lhs]
    ws = refs[n_lhs:2 * n_lhs]
    x_ref, g_ref, b_ref, of_ref, ob_ref = refs[2 * n_lhs:]
    for r in range(0, of_ref.shape[0], LN_SUB_ROWS):
        rows = slice(r, r + LN_SUB_ROWS)
        acc = _dot(lhs[0][rows, :], ws[0][...])
        for l_ref, w_ref in zip(lhs[1:], ws[1:]):
            acc = acc + _dot(l_ref[rows, :], w_ref[...])
        y = _layer_norm(ALPHA * x_ref[rows, :] + acc, g_ref[...], b_ref[...])
        of_ref[rows, :] = y
        ob_ref[rows, :] = y.astype(BF16)


def _out_ln(lhs_list, w, xf, g, b, tm, name):
    m, n = xf.shape
    row = lambda i: (i, 0)
    in_specs, args, off = [], [], 0
    for l in lhs_list:
        in_specs.append(pl.BlockSpec((tm, l.shape[1]), row))
        args.append(l)
    for l in lhs_list:
        kl = l.shape[1]
        in_specs.append(pl.BlockSpec((kl, n), functools.partial(lambda i, o: (o, 0), o=off // kl)))
        args.append(w)
        off += kl
    in_specs += [pl.BlockSpec((tm, n), row), pl.BlockSpec((1, n), lambda i: (0, 0)), pl.BlockSpec((1, n), lambda i: (0, 0))]
    args += [xf, g, b]
    return pl.pallas_call(
        functools.partial(_out_ln_kernel, n_lhs=len(lhs_list)),
        grid=(m // tm,),
        in_specs=in_specs,
        out_specs=[pl.BlockSpec((tm, n), row), pl.BlockSpec((tm, n), row)],
        out_shape=[jax.ShapeDtypeStruct((m, n), F32), jax.ShapeDtypeStruct((m, n), BF16)],
        compiler_params=_cparams(("parallel",), 56),
        name=name,
    )(*args)


def _dsa_kernel(q_ref, k_ref, v_ref, qi_ref, ki_ref, w_ref, o_ref, plane_ref, bias_ref, *, tq, seq, topk):
    i = pl.program_id(1)
    per_span = DSA_KEY_SPAN // tq
    for n in range(1, seq // DSA_KEY_SPAN + 1):
        pl.when(i // per_span == n - 1)(functools.partial(
            _dsa_body, i, q_ref, k_ref, v_ref, qi_ref, ki_ref, w_ref, o_ref, plane_ref, bias_ref,
            tq=tq, nk=n * DSA_KEY_SPAN, topk=topk))


def _dsa_body(i, q_ref, k_ref, v_ref, qi_ref, ki_ref, w_ref, o_ref, plane_ref, bias_ref, *, tq, nk, topk):
    span = DSA_KEY_SPAN
    n_groups = nk // BIT_GROUP
    w_t = w_ref[...].T
    key_pos = nk - span + lax.broadcasted_iota(jnp.int32, (span, tq), 0)
    q_pos = i * tq + lax.broadcasted_iota(jnp.int32, (span, tq), 1)
    allowed = (key_pos >> 6) <= (q_pos >> 6)
    qi_all = jnp.concatenate([qi_ref[:, h * IDX_DIM:(h + 1) * IDX_DIM] for h in range(IDX_HEADS)], axis=0)
    for c in range(nk // span):
        rel = jnp.maximum(_dot_nt(ki_ref[c * span:(c + 1) * span, :IDX_DIM], qi_all), 0.0)
        score = jnp.zeros((span, tq), F32)
        for h in range(IDX_HEADS):
            score = score + rel[:, h * tq:(h + 1) * tq] * w_t[W_IDX_LANE + h:W_IDX_LANE + h + 1, :]
        if c == nk // span - 1:
            score = jnp.where(allowed, score, NEG_INF)
        bits = lax.bitcast_convert_type(score, jnp.int32)
        u = bits ^ ((bits >> 31) | INT_MIN)
        for gg in range(span // BIT_GROUP):
            words = [u[gg * BIT_GROUP + 8 * j:gg * BIT_GROUP + 8 * j + 8, :] for j in range(32)]
            g = c * (span // BIT_GROUP) + gg
            for b, plane in enumerate(_bit_transpose32(words)):
                plane_ref[g, b] = plane

    def select_bit(j, state):
        alive, keep, above = state[:n_groups], state[n_groups:2 * n_groups], state[-1]
        ones = [a & plane_ref[g, 31 - j] for g, a in enumerate(alive)]
        cnt = lax.population_count(ones[0])
        for o in ones[1:]:
            cnt = cnt + lax.population_count(o)
        for shift in (4, 2, 1):
            cnt = cnt + pltpu.roll(cnt, shift, 0)
        take = (above + cnt) >= topk
        zeros = [a ^ o for a, o in zip(alive, ones)]
        new_alive = [jnp.where(take, o, z) for o, z in zip(ones, zeros)]
        new_keep = [jnp.where(take, kp ^ z, kp) for kp, z in zip(keep, zeros)]
        return tuple(new_alive) + tuple(new_keep) + (jnp.where(take, above, above + cnt),)

    everything = jnp.full((8, tq), -1, jnp.int32)
    state = lax.fori_loop(0, 32, select_bit, (everything,) * (2 * n_groups) + (jnp.zeros((8, tq), jnp.int32),))
    alive, keep, above = state[:n_groups], state[n_groups:2 * n_groups], state[-1]

    need = topk - above
    sub = lax.broadcasted_iota(jnp.int32, (8, tq), 0)

    def below(limit, g):
        n_bits = jnp.clip((limit - g * BIT_GROUP - sub + 7) >> 3, 0, 32)
        return jnp.where(n_bits >= 32, -1, lax.shift_left(jnp.int32(1), jnp.minimum(n_bits, 31)) - 1)

    bound = jnp.zeros((8, tq), jnp.int32)
    step = 1 << ((nk - 1).bit_length() - 1)
    while step >= 1:
        cnt = lax.population_count(alive[0] & below(bound + step, 0))
        for g in range(1, n_groups):
            cnt = cnt + lax.population_count(alive[g] & below(bound + step, g))
        for shift in (4, 2, 1):
            cnt = cnt + pltpu.roll(cnt, shift, 0)
        bound = jnp.where(cnt < need, bound + step, bound)
        step //= 2
    keep = [kp ^ (a & ~below(bound + 1, g)) for g, (kp, a) in enumerate(zip(keep, alive))]

    for g in range(n_groups):
        for j in range(32):
            rows = slice(g * BIT_GROUP + 8 * j, g * BIT_GROUP + 8 * j + 8)
            bias = jnp.where(lax.shift_left(keep[g], 31 - j) < 0, 0.0, NEG_INF)
            if g * BIT_GROUP + 8 * j >= nk - span:
                bias = jnp.where(allowed[rows.start - (nk - span):rows.stop - (nk - span), :], bias, NEG_INF)
            bias_ref[rows, :] = bias

    k = k_ref[:nk, :]
    v_ext = jnp.concatenate([v_ref[:nk, :], jnp.ones((nk, HEAD_DIM), BF16)], axis=1)
    groups = [[slice(h * HEAD_DIM, (h + 1) * HEAD_DIM) for h in range(g * DSA_HEAD_GROUP, (g + 1) * DSA_HEAD_GROUP)]
              for g in range(N_HEADS_A // DSA_HEAD_GROUP)]
    logits = [_dot_nt(k, jnp.concatenate([q_ref[:, cols] for cols in heads], axis=0)) * SCALE for heads in groups]
    probs = []
    for lg in logits:
        lg = jnp.concatenate([lg[:, j * tq:(j + 1) * tq] + bias_ref[:nk, :] for j in range(DSA_HEAD_GROUP)], axis=1)
        probs.append(jnp.exp(lg - jnp.max(lg, axis=0, keepdims=True)).astype(BF16))
    outs = [lax.dot_general(e, v_ext, (((0,), (0,)), ((), ())), preferred_element_type=F32) for e in probs]
    for heads, o in zip(groups, outs):
        o = o[:, :HEAD_DIM] / o[:, HEAD_DIM:HEAD_DIM + 1]
        for j, cols in enumerate(heads):
            o_ref[:, cols] = o[j * tq:(j + 1) * tq].astype(BF16)


def _bit_transpose32(words):
    words = list(words)
    for dist, low in ((16, 0x0000FFFF), (8, 0x00FF00FF), (4, 0x0F0F0F0F), (2, 0x33333333), (1, 0x55555555)):
        for k in range(32):
            if k & dist:
                continue
            a, b = words[k], words[k + dist]
            t = (lax.shift_right_logical(a, dist) ^ b) & low
            words[k + dist] = b ^ t
            words[k] = a ^ lax.shift_left(t, dist)
    return words


def _dsa_attention(qkv_a, qk_i, w_i, batch, seq):
    tq = 2 * CHUNK
    nq = seq // tq
    hq = N_HEADS_A * HEAD_DIM
    kcol = hq // LANE
    icol = (IDX_HEADS * IDX_DIM) // LANE
    qmap = lambda b, i: (b * nq + i, 0)
    return pl.pallas_call(
        functools.partial(_dsa_kernel, tq=tq, seq=seq, topk=min(TOPK_MAX, seq // 4)),
        grid=(batch, nq),
        in_specs=[pl.BlockSpec((tq, hq), qmap),
                  pl.BlockSpec((seq, LANE), lambda b, i: (b, kcol)),
                  pl.BlockSpec((seq, LANE), lambda b, i: (b, kcol + 1)),
                  pl.BlockSpec((tq, IDX_HEADS * IDX_DIM), qmap),
                  pl.BlockSpec((seq, LANE), lambda b, i: (b, icol)),
                  pl.BlockSpec((tq, LANE), qmap)],
        out_specs=pl.BlockSpec((tq, hq), qmap),
        out_shape=jax.ShapeDtypeStruct((batch * seq, hq), BF16),
        scratch_shapes=[pltpu.VMEM((seq // BIT_GROUP, 32, 8, tq), jnp.int32), pltpu.VMEM((seq, tq), F32)],
        compiler_params=_cparams(("parallel", "parallel"), 48),
        name="dsa_attention",
    )(qkv_a, qkv_a, qkv_a, qk_i, qk_i, w_i)


def _band_bias_kernel(rel_ref, o_ref):
    h = pl.program_id(0)
    i = lax.broadcasted_iota(jnp.int32, (BAND_Q, BAND_Q), 0)
    jj = lax.broadcasted_iota(jnp.int32, (BAND_Q, BAND_Q), 1)
    for jb in range(BAND_KB):
        j = jb * BAND_Q + jj
        idx = jnp.clip(i - j + LEFT_CHUNKS * CHUNK, -(CHUNK - 1), REL_CLIP) + (CHUNK - 1)
        cq = i >> 6
        ck = j >> 6
        d_lo = LEFT_CHUNKS * CHUNK - (jb + 1) * BAND_Q + 1
        d_hi = LEFT_CHUNKS * CHUNK - jb * BAND_Q + BAND_Q - 1
        r_lo = min(max(d_lo, -(CHUNK - 1)), REL_CLIP) + (CHUNK - 1)
        r_hi = min(max(d_hi, -(CHUNK - 1)), REL_CLIP) + (CHUNK - 1)
        val = lax.fori_loop(r_lo, r_hi + 1, lambda r, acc: jnp.where(idx == r, rel_ref[h, r], acc),
                            jnp.zeros((BAND_Q, BAND_Q), F32))
        val = jnp.where(ck >= cq, val, NEG_INF)
        o_ref[0, jb] = jnp.where(ck <= cq + LEFT_CHUNKS, val, NEG_INF)


def _band_bias(rel_bias):
    nh = rel_bias.shape[0]
    return pl.pallas_call(
        _band_bias_kernel,
        grid=(nh,),
        in_specs=[pl.BlockSpec(memory_space=pltpu.SMEM)],
        out_specs=pl.BlockSpec((1, BAND_KB, BAND_Q, BAND_Q), lambda h: (h, 0, 0, 0)),
        out_shape=jax.ShapeDtypeStruct((nh, BAND_KB, BAND_Q, BAND_Q), F32),
        name="band_bias",
    )(rel_bias)


def _band_kernel(q_ref, k_ref, v_ref, bias_ref, o_ref):
    p = pl.program_id(1)
    first = BAND_KB - 1
    shift = jnp.maximum(first - p, 0)
    s0 = pl.multiple_of(jnp.maximum(p - first, 0) * BAND_Q, BAND_Q)
    heads = [slice(h * HEAD_DIM, (h + 1) * HEAD_DIM) for h in range(N_HEADS_B)]
    logits = [_dot_nt(q_ref[:, cols], k_ref[pl.ds(s0, BAND_W), cols]) * SCALE for cols in heads]
    probs, sums = [], []
    for h, lg in enumerate(logits):
        blocks = []
        for jb in range(BAND_KB):
            src = jb + shift
            blk = lg[:, jb * BAND_Q:(jb + 1) * BAND_Q] + bias_ref[h, jnp.minimum(src, first)]
            blocks.append(jnp.where(src <= first, blk, NEG_INF))
        lg = jnp.concatenate(blocks, axis=1)
        e = jnp.exp(lg - jnp.max(lg, axis=1, keepdims=True))
        sums.append(jnp.sum(e, axis=1, keepdims=True))
        probs.append(e.astype(BF16))
    outs = [_dot(e, v_ref[pl.ds(s0, BAND_W), cols]) for e, cols in zip(probs, heads)]
    for o, s, cols in zip(outs, sums, heads):
        o_ref[:, cols] = (o / s).astype(BF16)


def _band_attention(qkv_b, bias, batch, seq):
    nq = seq // BAND_Q
    hq = N_HEADS_B * HEAD_DIM
    qmap = lambda b, p: (b * nq + p, 0)
    return pl.pallas_call(
        _band_kernel,
        grid=(batch, nq),
        in_specs=[pl.BlockSpec((BAND_Q, hq), qmap),
                  pl.BlockSpec((seq, hq), lambda b, p: (b, 1)),
                  pl.BlockSpec((seq, hq), lambda b, p: (b, 2)),
                  pl.BlockSpec(bias.shape, lambda b, p: (0, 0, 0, 0))],
        out_specs=pl.BlockSpec((BAND_Q, hq), qmap),
        out_shape=jax.ShapeDtypeStruct((batch * seq, hq), BF16),
        compiler_params=_cparams(("parallel", "parallel"), 48),
        name="band_attention",
    )(qkv_b, qkv_b, qkv_b, bias)


def _sb_blocks(qs, k_blks, v_blks, later, carry, past):
    zs = [_dot_nt(q, k) * (SCALE * LOG2_E) for q, k in zip(qs, k_blks)]
    log_betas, drops, splits = [], [], []
    for z2 in zs:
        neg_abs = lax.bitcast_convert_type(lax.bitcast_convert_type(z2, jnp.int32) | INT_MIN, F32)
        drop = jnp.maximum(z2, 0.0) + jnp.log2(1.0 + jnp.exp2(neg_abs))
        log_betas.append(z2 - drop)
        if past is not None:
            drop = jnp.where(past, drop, 0.0)
        drops.append(drop)
        hi = drop.astype(BF16)
        r1 = drop - hi.astype(F32)
        mid = r1.astype(BF16)
        lo = (r1 - mid.astype(F32)).astype(BF16)
        splits.append(jnp.concatenate([hi, mid, lo], axis=1))
    afters = [_dot(s, later) for s in splits]
    probs, tails = [], []
    for h, (log_beta, drop, after) in enumerate(zip(log_betas, drops, afters)):
        row_sum = jnp.sum(drop, axis=1, keepdims=True)
        if carry is None:
            a = jnp.exp2(log_beta - after)
            tails.append(row_sum)
        else:
            a = jnp.exp2(log_beta - carry[2 * h] - after)
            tails.append(carry[2 * h] + row_sum)
        if past is not None:
            a = jnp.where(past, a, 0.0)
        probs.append(a.astype(BF16))
    pvs = [_dot(a, v) for a, v in zip(probs, v_blks)]
    out = []
    for h, (tail, pv) in enumerate(zip(tails, pvs)):
        out += [tail, pv if carry is None else carry[2 * h + 1] + pv]
    return tuple(out)


def _sb_kernel(q_ref, k_ref, v_ref, o_ref, *, tq, nh):
    i = pl.program_id(2)
    r3 = lax.broadcasted_iota(jnp.int32, (3 * tq, tq), 0)
    c3 = lax.broadcasted_iota(jnp.int32, (3 * tq, tq), 1)
    later = jnp.where((r3 & (tq - 1)) > c3, 1.0, 0.0).astype(BF16)
    row = lax.broadcasted_iota(jnp.int32, (tq, tq), 0)
    col = lax.broadcasted_iota(jnp.int32, (tq, tq), 1)
    heads = [slice(h * HEAD_DIM, (h + 1) * HEAD_DIM) for h in range(nh)]

    def walk(s0, carry, past):
        return _sb_blocks([q_ref[:, cols] for cols in heads],
                          [k_ref[pl.ds(s0, tq), cols] for cols in heads],
                          [v_ref[pl.ds(s0, tq), cols] for cols in heads], later, carry, past)

    def least_tail(c):
        m = c[0]
        for h in range(1, nh):
            m = jnp.minimum(m, c[2 * h])
        return jnp.min(m)

    def alive(state):
        return jnp.logical_and(state[0] < i, state[1] < SB_DEAD_BITS)

    def step(state):
        c = walk(pl.multiple_of((i - 1 - state[0]) * tq, tq), state[2:], None)
        return (state[0] + 1, least_tail(c)) + c

    carry = walk(pl.multiple_of(i * tq, tq), None, col < row)
    carry = lax.while_loop(alive, step, (jnp.int32(0), least_tail(carry)) + carry)[2:]
    for h, cols in enumerate(heads):
        o_ref[:, cols] = carry[2 * h + 1].astype(BF16)


def _stick_breaking(qkv, batch, seq, tq, nh):
    nq = seq // tq
    ng = N_HEADS_C // nh
    w = nh * HEAD_DIM
    qmap = lambda b, g, i: (b * nq + i, g)
    return pl.pallas_call(
        functools.partial(_sb_kernel, tq=tq, nh=nh),
        grid=(batch, ng, nq),
        in_specs=[pl.BlockSpec((tq, w), qmap),
                  pl.BlockSpec((seq, w), lambda b, g, i: (b, ng + g)),
                  pl.BlockSpec((seq, w), lambda b, g, i: (b, 2 * ng + g))],
        out_specs=pl.BlockSpec((tq, w), qmap),
        out_shape=jax.ShapeDtypeStruct((batch * seq, N_HEADS_C * HEAD_DIM), BF16),
        compiler_params=_cparams(("parallel", "parallel", "parallel"), 32),
        name="stick_breaking",
    )(qkv, qkv, qkv)


def _xattn_kernel(xb_ref, xf_ref, kv_ref, wq_ref, wo_ref, g_ref, b_ref, of_ref):
    hk = N_HEADS_X * HEAD_DIM
    heads = [slice(h * HEAD_DIM, (h + 1) * HEAD_DIM) for h in range(N_HEADS_X)]
    for r in range(0, of_ref.shape[0], LN_SUB_ROWS):
        rows = slice(r, r + LN_SUB_ROWS)
        q = _dot(xb_ref[rows, :], wq_ref[...]).astype(BF16)
        logits = [_dot_nt(q[:, cols], kv_ref[:, cols]) * SCALE for cols in heads]
        probs, sums = [], []
        for lg in logits:
            e = jnp.exp(lg - jnp.max(lg, axis=1, keepdims=True))
            sums.append(jnp.sum(e, axis=1, keepdims=True))
            probs.append(e.astype(BF16))
        outs = [_dot(e, kv_ref[:, hk + h * HEAD_DIM:hk + (h + 1) * HEAD_DIM]) for h, e in enumerate(probs)]
        o = jnp.concatenate([(oh / s).astype(BF16) for oh, s in zip(outs, sums)], axis=1)
        of_ref[rows, :] = _layer_norm(ALPHA * xf_ref[rows, :] + _dot(o, wo_ref[...]), g_ref[...], b_ref[...])


def _xattn(xb, xf, kv, wq, wo, g, b, seq, tm):
    m, n = xf.shape
    per_seq = seq // tm
    row = lambda i: (i, 0)
    const = lambda i: (0, 0)
    return pl.pallas_call(
        _xattn_kernel,
        grid=(m // tm,),
        in_specs=[pl.BlockSpec((tm, n), row), pl.BlockSpec((tm, n), row),
                  pl.BlockSpec((MEM_LEN, kv.shape[1]), lambda i: (i // per_seq, 0)),
                  pl.BlockSpec(wq.shape, const), pl.BlockSpec(wo.shape, const),
                  pl.BlockSpec((1, n), const), pl.BlockSpec((1, n), const)],
        out_specs=pl.BlockSpec((tm, n), row),
        out_shape=jax.ShapeDtypeStruct((m, n), F32),
        compiler_params=_cparams(("parallel",), 56),
        name="xattn_ln",
    )(xb, xf, kv, wq, wo, g, b)


def _rope_tables(seq):
    pos = jnp.arange(seq).astype(F32)[:, None]

    def half_tables(half):
        inv_freq = ROPE_THETA ** (-jnp.arange(half, dtype=F32) / half)
        ang = pos * inv_freq[None, :]
        return jnp.cos(ang), jnp.sin(ang)

    c64, s64 = half_tables(HEAD_DIM // 2)
    c32, s32 = half_tables(IDX_DIM // 2)
    head = (jnp.concatenate([c64, c64], 1), jnp.concatenate([-s64, s64], 1))
    idx = (jnp.concatenate([c32, c32, c32, c32], 1), jnp.concatenate([-s32, s32, -s32, s32], 1))
    rest = LANE - IDX_DIM
    idx_key = (jnp.concatenate([c32, c32, jnp.ones((seq, rest), F32)], 1),
               jnp.concatenate([-s32, s32, jnp.zeros((seq, rest), F32)], 1))
    return head, idx, idx_key


def kernel(x, mem, ln_g, ln_b, ffn_in, ffn_out, xattn_q, xattn_kv, xattn_o,
           even_in, even_out, even_rel_bias, odd_in, odd_out):
    batch, seq, d = x.shape
    t = batch * seq
    xf = x.reshape(t, d)
    memb = mem.reshape(batch * MEM_LEN, d).astype(BF16)
    tab_head, tab_idx, tab_idx_key = _rope_tables(seq)

    modes_a = ((0, HEAD_DIM),) * (N_HEADS_A + 1) + (None,)
    modes_i = ((0, IDX_DIM),) * ((IDX_HEADS * IDX_DIM) // LANE) + ((1, IDX_DIM),)

    def ln_params(layer, j):
        return ln_g[layer, j][None, :], ln_b[layer, j][None, :]

    def ffn(xf, layer, j):
        g, b = ln_params(layer, 3 * j)
        outs = _ffn(xf, ffn_in, ffn_out, (layer, j), g, b, 1024, 256, 0.5, want_bf16=(j == 0))
        return outs[0], (outs[1] if j == 0 else None)

    for layer in range(DEPTH):
        xf, xb = ffn(xf, layer, 0)

        g, b = ln_params(layer, 1)
        if layer % 2 == 0:
            w_in = even_in[layer // 2]
            w_a = w_in[:, :A_COLS].astype(BF16)
            w_i = jnp.pad(w_in[:, A_COLS:A_COLS + I_COLS], ((0, 0), (0, I_PAD - I_COLS))).astype(BF16)
            w_b = w_in[:, A_COLS + I_COLS:].astype(BF16)
            qkv_a = _proj_rope(xb, w_a, [tab_head], modes_a, seq, 512, False, "proj_dsa")[0]
            qk_i, w_idx = _proj_rope(xb, w_i, [tab_idx, tab_idx_key], modes_i, seq, 512, True, "proj_idx")
            qkv_b = _matmul(xb, w_b, 2048, 1024, "proj_band")
            o_a = _dsa_attention(qkv_a, qk_i, w_idx, batch, seq)
            o_b = _band_attention(qkv_b, _band_bias(even_rel_bias[layer // 2]), batch, seq)
            xf, xb = _out_ln([o_a, o_b], even_out[layer // 2].astype(BF16), xf, g, b, 512, "even_out_ln")
        else:
            qkv = _matmul(xb, odd_in, 1024, 1024, "proj_odd", lead=(layer // 2,))
            o = _stick_breaking(qkv, batch, seq, 256, 4)
            xf, xb = _out_ln([o], odd_out[layer // 2].astype(BF16), xf, g, b, 512, "odd_out_ln")

        g, b = ln_params(layer, 2)
        kv = _matmul(memb, xattn_kv[layer].astype(BF16), 1024, 1024, "proj_mem")
        xf = _xattn(xb, xf, kv, xattn_q[layer].astype(BF16), xattn_o[layer].astype(BF16), g, b, seq, 512)

        xf, _ = ffn(xf, layer, 1)

    return xf.reshape(batch, seq, d)
```

```python
import functools

import jax
import jax.numpy as jnp
import numpy as np
from jax import lax
from jax.experimental import pallas as pl
from jax.experimental.pallas import tpu as pltpu

F32 = jnp.float32
BF16 = jnp.bfloat16

D_MODEL = 2048
DEPTH = 4
CHUNK = 64
MEM_LEN = 256
HEAD_DIM = 128
ROPE_THETA = 10000.0
LN_EPS = 1e-5
NEG_INF = -1e30
N_HEADS_A = 8
N_HEADS_B = 8
IDX_HEADS = 16
IDX_DIM = 64
TOPK_MAX = 256
LEFT_CHUNKS = 8
REL_CLIP = 128
REL_SIZE = CHUNK + REL_CLIP
N_HEADS_C = 16
N_HEADS_X = 4
FFN_DIM = ((8 * D_MODEL // 3 + 255) // 256) * 256
ALPHA = (2.0 * DEPTH) ** 0.25
SCALE = HEAD_DIM ** -0.5
LOG2_E = 1.4426950408889634

LANE = 128
INT_MIN = -2147483648
MIB = 1024 * 1024

A_COLS = (N_HEADS_A + 2) * HEAD_DIM
I_COLS = IDX_HEADS * IDX_DIM + IDX_DIM + IDX_HEADS
I_PAD = ((I_COLS + LANE - 1) // LANE) * LANE
W_IDX_LANE = IDX_DIM
B_COLS = 3 * N_HEADS_B * HEAD_DIM

BAND_Q = 2 * CHUNK
BAND_KB = (LEFT_CHUNKS * CHUNK) // BAND_Q + 1
BAND_W = BAND_KB * BAND_Q

LN_SUB_ROWS = 256

SB_DEAD_BITS = 152.0

DSA_KEY_SPAN = 256
DSA_HEAD_GROUP = 4
BIT_GROUP = 32 * 8
DSA_ROW_GROUP = 32


def _cparams(sem, vmem_mib):
    return pltpu.CompilerParams(dimension_semantics=sem, vmem_limit_bytes=vmem_mib * MIB)


def _dot(a, b):
    return jnp.dot(a, b, preferred_element_type=F32)


def _dot_nt(a, b):
    return lax.dot_general(a, b, (((1,), (1,)), ((), ())), preferred_element_type=F32)


def _layer_norm(y, g, b):
    mu = jnp.mean(y, axis=-1, keepdims=True)
    d = y - mu
    var = jnp.mean(d * d, axis=-1, keepdims=True)
    return d * lax.rsqrt(var + LN_EPS) * g + b


def _mm_kernel(x_ref, w_ref, o_ref, *scratch):
    if scratch:
        wb_ref, = scratch

        @pl.when(pl.program_id(1) == 0)
        def _():
            wb_ref[...] = w_ref[...].astype(BF16)
    else:
        wb_ref = w_ref
    o_ref[...] = _dot(x_ref[...], wb_ref[...]).astype(o_ref.dtype)


def _matmul(x, w, tm, tn, name, lead=()):
    m, k = x.shape
    n = w.shape[-1]
    cast = w.dtype != BF16
    return pl.pallas_call(
        _mm_kernel,
        grid=(n // tn, m // tm),
        in_specs=[pl.BlockSpec((tm, k), lambda j, i: (i, 0)),
                  pl.BlockSpec((None,) * len(lead) + (k, tn), lambda j, i: lead + (0, j))],
        out_specs=pl.BlockSpec((tm, tn), lambda j, i: (i, j)),
        out_shape=jax.ShapeDtypeStruct((m, n), BF16),
        scratch_shapes=[pltpu.VMEM((k, tn), BF16)] if cast else [],
        compiler_params=_cparams(("parallel", "arbitrary"), 48),
        name=name,
    )(x, w)


def _rope_group(x, cos, sin, width):
    if width == LANE:
        partner = pltpu.roll(x, LANE // 2, 1)
    else:
        lane = lax.broadcasted_iota(jnp.int32, x.shape, 1)
        half = width // 2
        partner = jnp.where((lane & half) == 0, pltpu.roll(x, LANE - half, 1), pltpu.roll(x, half, 1))
    return x * cos + partner * sin


def _proj_rope_kernel(x_ref, w_ref, *rest, modes, n_tab, f32_tail):
    tabs = rest[:2 * n_tab]
    outs = rest[2 * n_tab:]
    acc = _dot(x_ref[...], w_ref[...])
    for g, mode in enumerate(modes):
        blk = acc[:, g * LANE:(g + 1) * LANE]
        if mode is not None:
            tid, width = mode
            blk = _rope_group(blk, tabs[2 * tid][...], tabs[2 * tid + 1][...], width)
        outs[0][:, g * LANE:(g + 1) * LANE] = blk.astype(BF16)
    if f32_tail:
        outs[1][...] = acc[:, -LANE:]


def _proj_rope(x, w, tabs, modes, seq, tm, f32_tail, name):
    m, k = x.shape
    n = w.shape[1]
    per_seq = seq // tm
    in_specs = [pl.BlockSpec((tm, k), lambda i: (i, 0)), pl.BlockSpec((k, n), lambda i: (0, 0))]
    args = [x, w]
    for cos, sin in tabs:
        in_specs += [pl.BlockSpec((tm, LANE), lambda i: (i % per_seq, 0))] * 2
        args += [cos, sin]
    out_specs = [pl.BlockSpec((tm, n), lambda i: (i, 0))]
    out_shape = [jax.ShapeDtypeStruct((m, n), BF16)]
    if f32_tail:
        out_specs.append(pl.BlockSpec((tm, LANE), lambda i: (i, 0)))
        out_shape.append(jax.ShapeDtypeStruct((m, LANE), F32))
    return pl.pallas_call(
        functools.partial(_proj_rope_kernel, modes=modes, n_tab=len(tabs), f32_tail=f32_tail),
        grid=(m // tm,),
        in_specs=in_specs,
        out_specs=out_specs,
        out_shape=out_shape,
        compiler_params=_cparams(("parallel",), 48),
        name=name,
    )(*args)


def _ffn_kernel(x_ref, wa_ref, wg_ref, wo_ref, g_ref, b_ref, *rest, nf, tf, scale, want_bf16):
    if want_bf16:
        of_ref, ob_ref = rest
        xb_ref = ob_ref
    else:
        of_ref, xb_ref = rest
    j = pl.program_id(1)

    @pl.when(j == 0)
    def _():
        xb_ref[...] = x_ref[...].astype(BF16)

    subs = [slice(r, r + LN_SUB_ROWS) for r in range(0, of_ref.shape[0], LN_SUB_ROWS)]

    def hidden_slice(phase):
        w_in = jnp.concatenate([wa_ref[...], wg_ref[...]], axis=1).astype(BF16)
        wo = wo_ref[...].astype(BF16)
        ags = [_dot(xb_ref[rows, :], w_in) for rows in subs]
        hs = [(ag[:, :tf] * jax.nn.sigmoid(ag[:, :tf]) * ag[:, tf:]).astype(BF16) for ag in ags]
        parts = [_dot(h, wo) for h in hs]
        for rows, part in zip(subs, parts):
            if phase == "first":
                of_ref[rows, :] = part
            elif phase == "middle":
                of_ref[rows, :] += part
            else:
                y = _layer_norm(ALPHA * x_ref[rows, :] + scale * (of_ref[rows, :] + part), g_ref[...], b_ref[...])
                of_ref[rows, :] = y
                if want_bf16:
                    ob_ref[rows, :] = y.astype(BF16)

    pl.when(j == 0)(functools.partial(hidden_slice, "first"))
    pl.when(jnp.logical_and(j > 0, j < nf - 1))(functools.partial(hidden_slice, "middle"))
    pl.when(j == nf - 1)(functools.partial(hidden_slice, "last"))


def _ffn(xf, w_in, w_out, lead, g, b, tm, tf, scale, want_bf16):
    m, d = xf.shape
    nf = FFN_DIM // tf
    nl = (None,) * len(lead)
    row = lambda i, j: (i, 0)
    const = lambda i, j: (0, 0)
    out_specs = [pl.BlockSpec((tm, d), row)]
    out_shape = [jax.ShapeDtypeStruct((m, d), F32)]
    if want_bf16:
        out_specs.append(pl.BlockSpec((tm, d), row))
        out_shape.append(jax.ShapeDtypeStruct((m, d), BF16))
    return pl.pallas_call(
        functools.partial(_ffn_kernel, nf=nf, tf=tf, scale=scale, want_bf16=want_bf16),
        grid=(m // tm, nf),
        in_specs=[pl.BlockSpec((tm, d), row),
                  pl.BlockSpec(nl + (d, tf), lambda i, j: lead + (0, j)),
                  pl.BlockSpec(nl + (d, tf), lambda i, j: lead + (0, j + nf)),
                  pl.BlockSpec(nl + (tf, d), lambda i, j: lead + (j, 0)),
                  pl.BlockSpec((1, d), const), pl.BlockSpec((1, d), const)],
        out_specs=out_specs,
        out_shape=out_shape,
        scratch_shapes=[] if want_bf16 else [pltpu.VMEM((tm, d), BF16)],
        compiler_params=_cparams(("parallel", "arbitrary"), 60),
        name="ffn_fused",
    )(xf, w_in, w_in, w_out, g, b)


def _out_ln_kernel(*refs, n_lhs):
    lhs = refs[:n_lhs]
    ws = refs[n_lhs:2 * n_lhs]
    x_ref, g_ref, b_ref, of_ref, ob_ref = refs[2 * n_lhs:]
    for r in range(0, of_ref.shape[0], LN_SUB_ROWS):
        rows = slice(r, r + LN_SUB_ROWS)
        acc = _dot(lhs[0][rows, :], ws[0][...])
        for l_ref, w_ref in zip(lhs[1:], ws[1:]):
            acc = acc + _dot(l_ref[rows, :], w_ref[...])
        y = _layer_norm(ALPHA * x_ref[rows, :] + acc, g_ref[...], b_ref[...])
        of_ref[rows, :] = y
        ob_ref[rows, :] = y.astype(BF16)


def _out_ln(lhs_list, w, xf, g, b, tm, name):
    m, n = xf.shape
    row = lambda i: (i, 0)
    in_specs, args, off = [], [], 0
    for l in lhs_list:
        in_specs.append(pl.BlockSpec((tm, l.shape[1]), row))
        args.append(l)
    for l in lhs_list:
        kl = l.shape[1]
        in_specs.append(pl.BlockSpec((kl, n), functools.partial(lambda i, o: (o, 0), o=off // kl)))
        args.append(w)
        off += kl
    in_specs += [pl.BlockSpec((tm, n), row), pl.BlockSpec((1, n), lambda i: (0, 0)), pl.BlockSpec((1, n), lambda i: (0, 0))]
    args += [xf, g, b]
    return pl.pallas_call(
        functools.partial(_out_ln_kernel, n_lhs=len(lhs_list)),
        grid=(m // tm,),
        in_specs=in_specs,
        out_specs=[pl.BlockSpec((tm, n), row), pl.BlockSpec((tm, n), row)],
        out_shape=[jax.ShapeDtypeStruct((m, n), F32), jax.ShapeDtypeStruct((m, n), BF16)],
        compiler_params=_cparams(("parallel",), 56),
        name=name,
    )(*args)


def _dsa_kernel(q_ref, k_ref, v_ref, qi_ref, ki_ref, w_ref, o_ref, plane_ref, bias_ref, *, tq, seq, topk):
    i = pl.program_id(1)
    per_span = DSA_KEY_SPAN // tq
    for n in range(1, seq // DSA_KEY_SPAN + 1):
        pl.when(i // per_span == n - 1)(functools.partial(
            _dsa_body, i, q_ref, k_ref, v_ref, qi_ref, ki_ref, w_ref, o_ref, plane_ref, bias_ref,
            tq=tq, nk=n * DSA_KEY_SPAN, topk=topk))


def _dsa_body(i, q_ref, k_ref, v_ref, qi_ref, ki_ref, w_ref, o_ref, plane_ref, bias_ref, *, tq, nk, topk):
    span = DSA_KEY_SPAN
    n_groups = nk // BIT_GROUP
    w_t = w_ref[...].T
    key_pos = nk - span + lax.broadcasted_iota(jnp.int32, (span, tq), 0)
    q_pos = i * tq + lax.broadcasted_iota(jnp.int32, (span, tq), 1)
    allowed = (key_pos >> 6) <= (q_pos >> 6)
    qi_all = jnp.concatenate([qi_ref[:, h * IDX_DIM:(h + 1) * IDX_DIM] for h in range(IDX_HEADS)], axis=0)
    for c in range(nk // span):
        rel = jnp.maximum(_dot_nt(ki_ref[c * span:(c + 1) * span, :IDX_DIM], qi_all), 0.0)
        score = jnp.zeros((span, tq), F32)
        for h in range(IDX_HEADS):
            score = score + rel[:, h * tq:(h + 1) * tq] * w_t[W_IDX_LANE + h:W_IDX_LANE + h + 1, :]
        if c == nk // span - 1:
            score = jnp.where(allowed, score, NEG_INF)
        bits = lax.bitcast_convert_type(score, jnp.int32)
        u = bits ^ ((bits >> 31) | INT_MIN)
        for gg in range(span // BIT_GROUP):
            words = [u[gg * BIT_GROUP + 8 * j:gg * BIT_GROUP + 8 * j + 8, :] for j in range(32)]
            g = c * (span // BIT_GROUP) + gg
            for b, plane in enumerate(_bit_transpose32(words)):
                plane_ref[g, b] = plane

    def select_bit(j, state):
        alive, keep, above = state[:n_groups], state[n_groups:2 * n_groups], state[-1]
        ones = [a & plane_ref[g, 31 - j] for g, a in enumerate(alive)]
        cnt = lax.population_count(ones[0])
        for o in ones[1:]:
            cnt = cnt + lax.population_count(o)
        for shift in (4, 2, 1):
            cnt = cnt + pltpu.roll(cnt, shift, 0)
        take = (above + cnt) >= topk
        zeros = [a ^ o for a, o in zip(alive, ones)]
        new_alive = [jnp.where(take, o, z) for o, z in zip(ones, zeros)]
        new_keep = [jnp.where(take, kp ^ z, kp) for kp, z in zip(keep, zeros)]
        return tuple(new_alive) + tuple(new_keep) + (jnp.where(take, above, above + cnt),)

    everything = jnp.full((8, tq), -1, jnp.int32)
    state = lax.fori_loop(0, 32, select_bit, (everything,) * (2 * n_groups) + (jnp.zeros((8, tq), jnp.int32),))
    alive, keep, above = state[:n_groups], state[n_groups:2 * n_groups], state[-1]

    need = topk - above
    sub = lax.broadcasted_iota(jnp.int32, (8, tq), 0)

    def below(limit, g):
        n_bits = jnp.clip((limit - g * BIT_GROUP - sub + 7) >> 3, 0, 32)
        return jnp.where(n_bits >= 32, -1, lax.shift_left(jnp.int32(1), jnp.minimum(n_bits, 31)) - 1)

    bound = jnp.zeros((8, tq), jnp.int32)
    step = 1 << ((nk - 1).bit_length() - 1)
    while step >= 1:
        cnt = lax.population_count(alive[0] & below(bound + step, 0))
        for g in range(1, n_groups):
            cnt = cnt + lax.population_count(alive[g] & below(bound + step, g))
        for shift in (4, 2, 1):
            cnt = cnt + pltpu.roll(cnt, shift, 0)
        bound = jnp.where(cnt < need, bound + step, bound)
        step //= 2
    keep = [kp ^ (a & ~below(bound + 1, g)) for g, (kp, a) in enumerate(zip(keep, alive))]

    for g in range(n_groups):
        for j in range(32):
            rows = slice(g * BIT_GROUP + 8 * j, g * BIT_GROUP + 8 * j + 8)
            bias = jnp.where(lax.shift_left(keep[g], 31 - j) < 0, 0.0, NEG_INF)
            if g * BIT_GROUP + 8 * j >= nk - span:
                bias = jnp.where(allowed[rows.start - (nk - span):rows.stop - (nk - span), :], bias, NEG_INF)
            bias_ref[rows, :] = bias

    k = k_ref[:nk, :]
    v_ext = jnp.concatenate([v_ref[:nk, :], jnp.ones((nk, HEAD_DIM), BF16)], axis=1)
    groups = [[slice(h * HEAD_DIM, (h + 1) * HEAD_DIM) for h in range(g * DSA_HEAD_GROUP, (g + 1) * DSA_HEAD_GROUP)]
              for g in range(N_HEADS_A // DSA_HEAD_GROUP)]
    logits = [_dot_nt(k, jnp.concatenate([q_ref[:, cols] for cols in heads], axis=0)) * SCALE for heads in groups]
    probs = []
    for lg in logits:
        lg = jnp.concatenate([lg[:, j * tq:(j + 1) * tq] + bias_ref[:nk, :] for j in range(DSA_HEAD_GROUP)], axis=1)
        probs.append(jnp.exp(lg - jnp.max(lg, axis=0, keepdims=True)).astype(BF16))
    outs = [lax.dot_general(e, v_ext, (((0,), (0,)), ((), ())), preferred_element_type=F32) for e in probs]
    for heads, o in zip(groups, outs):
        o = o[:, :HEAD_DIM] / o[:, HEAD_DIM:HEAD_DIM + 1]
        for j, cols in enumerate(heads):
            o_ref[:, cols] = o[j * tq:(j + 1) * tq].astype(BF16)


def _bit_transpose32(words):
    words = list(words)
    for dist, low in ((16, 0x0000FFFF), (8, 0x00FF00FF), (4, 0x0F0F0F0F), (2, 0x33333333), (1, 0x55555555)):
        for k in range(32):
            if k & dist:
                continue
            a, b = words[k], words[k + dist]
            t = (lax.shift_right_logical(a, dist) ^ b) & low
            words[k + dist] = b ^ t
            words[k] = a ^ lax.shift_left(t, dist)
    return words


def _dsa_attention(qkv_a, qk_i, w_i, batch, seq):
    tq = 2 * CHUNK
    nq = seq // tq
    hq = N_HEADS_A * HEAD_DIM
    kcol = hq // LANE
    icol = (IDX_HEADS * IDX_DIM) // LANE
    qmap = lambda b, i: (b * nq + i, 0)
    return pl.pallas_call(
        functools.partial(_dsa_kernel, tq=tq, seq=seq, topk=min(TOPK_MAX, seq // 4)),
        grid=(batch, nq),
        in_specs=[pl.BlockSpec((tq, hq), qmap),
                  pl.BlockSpec((seq, LANE), lambda b, i: (b, kcol)),
                  pl.BlockSpec((seq, LANE), lambda b, i: (b, kcol + 1)),
                  pl.BlockSpec((tq, IDX_HEADS * IDX_DIM), qmap),
                  pl.BlockSpec((seq, LANE), lambda b, i: (b, icol)),
                  pl.BlockSpec((tq, LANE), qmap)],
        out_specs=pl.BlockSpec((tq, hq), qmap),
        out_shape=jax.ShapeDtypeStruct((batch * seq, hq), BF16),
        scratch_shapes=[pltpu.VMEM((seq // BIT_GROUP, 32, 8, tq), jnp.int32), pltpu.VMEM((seq, tq), F32)],
        compiler_params=_cparams(("parallel", "parallel"), 48),
        name="dsa_attention",
    )(qkv_a, qkv_a, qkv_a, qk_i, qk_i, w_i)


def _band_bias_kernel(rel_ref, o_ref):
    h = pl.program_id(0)
    i = lax.broadcasted_iota(jnp.int32, (BAND_Q, BAND_Q), 0)
    jj = lax.broadcasted_iota(jnp.int32, (BAND_Q, BAND_Q), 1)
    for jb in range(BAND_KB):
        j = jb * BAND_Q + jj
        idx = jnp.clip(i - j + LEFT_CHUNKS * CHUNK, -(CHUNK - 1), REL_CLIP) + (CHUNK - 1)
        cq = i >> 6
        ck = j >> 6
        d_lo = LEFT_CHUNKS * CHUNK - (jb + 1) * BAND_Q + 1
        d_hi = LEFT_CHUNKS * CHUNK - jb * BAND_Q + BAND_Q - 1
        r_lo = min(max(d_lo, -(CHUNK - 1)), REL_CLIP) + (CHUNK - 1)
        r_hi = min(max(d_hi, -(CHUNK - 1)), REL_CLIP) + (CHUNK - 1)
        val = lax.fori_loop(r_lo, r_hi + 1, lambda r, acc: jnp.where(idx == r, rel_ref[h, r], acc),
                            jnp.zeros((BAND_Q, BAND_Q), F32))
        val = jnp.where(ck >= cq, val, NEG_INF)
        o_ref[0, jb] = jnp.where(ck <= cq + LEFT_CHUNKS, val, NEG_INF)


def _band_bias(rel_bias):
    nh = rel_bias.shape[0]
    return pl.pallas_call(
        _band_bias_kernel,
        grid=(nh,),
        in_specs=[pl.BlockSpec(memory_space=pltpu.SMEM)],
        out_specs=pl.BlockSpec((1, BAND_KB, BAND_Q, BAND_Q), lambda h: (h, 0, 0, 0)),
        out_shape=jax.ShapeDtypeStruct((nh, BAND_KB, BAND_Q, BAND_Q), F32),
        name="band_bias",
    )(rel_bias)


def _band_kernel(q_ref, k_ref, v_ref, bias_ref, o_ref):
    p = pl.program_id(1)
    first = BAND_KB - 1
    shift = jnp.maximum(first - p, 0)
    s0 = pl.multiple_of(jnp.maximum(p - first, 0) * BAND_Q, BAND_Q)
    heads = [slice(h * HEAD_DIM, (h + 1) * HEAD_DIM) for h in range(N_HEADS_B)]
    logits = [_dot_nt(q_ref[:, cols], k_ref[pl.ds(s0, BAND_W), cols]) * SCALE for cols in heads]
    probs, sums = [], []
    for h, lg in enumerate(logits):
        blocks = []
        for jb in range(BAND_KB):
            src = jb + shift
            blk = lg[:, jb * BAND_Q:(jb + 1) * BAND_Q] + bias_ref[h, jnp.minimum(src, first)]
            blocks.append(jnp.where(src <= first, blk, NEG_INF))
        lg = jnp.concatenate(blocks, axis=1)
        e = jnp.exp(lg - jnp.max(lg, axis=1, keepdims=True))
        sums.append(jnp.sum(e, axis=1, keepdims=True))
        probs.append(e.astype(BF16))
    outs = [_dot(e, v_ref[pl.ds(s0, BAND_W), cols]) for e, cols in zip(probs, heads)]
    for o, s, cols in zip(outs, sums, heads):
        o_ref[:, cols] = (o / s).astype(BF16)


def _band_attention(qkv_b, bias, batch, seq):
    nq = seq // BAND_Q
    hq = N_HEADS_B * HEAD_DIM
    qmap = lambda b, p: (b * nq + p, 0)
    return pl.pallas_call(
        _band_kernel,
        grid=(batch, nq),
        in_specs=[pl.BlockSpec((BAND_Q, hq), qmap),
                  pl.BlockSpec((seq, hq), lambda b, p: (b, 1)),
                  pl.BlockSpec((seq, hq), lambda b, p: (b, 2)),
                  pl.BlockSpec(bias.shape, lambda b, p: (0, 0, 0, 0))],
        out_specs=pl.BlockSpec((BAND_Q, hq), qmap),
        out_shape=jax.ShapeDtypeStruct((batch * seq, hq), BF16),
        compiler_params=_cparams(("parallel", "parallel"), 48),
        name="band_attention",
    )(qkv_b, qkv_b, qkv_b, bias)


def _sb_blocks(qs, k_blks, v_blks, later, carry, past):
    zs = [_dot_nt(q, k) * (SCALE * LOG2_E) for q, k in zip(qs, k_blks)]
    log_betas, drops, splits = [], [], []
    for z2 in zs:
        neg_abs = lax.bitcast_convert_type(lax.bitcast_convert_type(z2, jnp.int32) | INT_MIN, F32)
        drop = jnp.maximum(z2, 0.0) + jnp.log2(1.0 + jnp.exp2(neg_abs))
        log_betas.append(z2 - drop)
        if past is not None:
            drop = jnp.where(past, drop, 0.0)
        drops.append(drop)
        hi = drop.astype(BF16)
        r1 = drop - hi.astype(F32)
        mid = r1.astype(BF16)
        lo = (r1 - mid.astype(F32)).astype(BF16)
        splits.append(jnp.concatenate([hi, mid, lo], axis=1))
    afters = [_dot(s, later) for s in splits]
    probs, tails = [], []
    for h, (log_beta, drop, after) in enumerate(zip(log_betas, drops, afters)):
        row_sum = jnp.sum(drop, axis=1, keepdims=True)
        if carry is None:
            a = jnp.exp2(log_beta - after)
            tails.append(row_sum)
        else:
            a = jnp.exp2(log_beta - carry[2 * h] - after)
            tails.append(carry[2 * h] + row_sum)
        if past is not None:
            a = jnp.where(past, a, 0.0)
        probs.append(a.astype(BF16))
    pvs = [_dot(a, v) for a, v in zip(probs, v_blks)]
    out = []
    for h, (tail, pv) in enumerate(zip(tails, pvs)):
        out += [tail, pv if carry is None else carry[2 * h + 1] + pv]
    return tuple(out)


def _sb_kernel(q_ref, k_ref, v_ref, o_ref, *, tq, nh):
    i = pl.program_id(2)
    r3 = lax.broadcasted_iota(jnp.int32, (3 * tq, tq), 0)
    c3 = lax.broadcasted_iota(jnp.int32, (3 * tq, tq), 1)
    later = jnp.where((r3 & (tq - 1)) > c3, 1.0, 0.0).astype(BF16)
    row = lax.broadcasted_iota(jnp.int32, (tq, tq), 0)
    col = lax.broadcasted_iota(jnp.int32, (tq, tq), 1)
    heads = [slice(h * HEAD_DIM, (h + 1) * HEAD_DIM) for h in range(nh)]

    def walk(s0, carry, past):
        return _sb_blocks([q_ref[:, cols] for cols in heads],
                          [k_ref[pl.ds(s0, tq), cols] for cols in heads],
                          [v_ref[pl.ds(s0, tq), cols] for cols in heads], later, carry, past)

    def least_tail(c):
        m = c[0]
        for h in range(1, nh):
            m = jnp.minimum(m, c[2 * h])
        return jnp.min(m)

    def alive(state):
        return jnp.logical_and(state[0] < i, state[1] < SB_DEAD_BITS)

    def step(state):
        c = walk(pl.multiple_of((i - 1 - state[0]) * tq, tq), state[2:], None)
        return (state[0] + 1, least_tail(c)) + c

    carry = walk(pl.multiple_of(i * tq, tq), None, col < row)
    carry = lax.while_loop(alive, step, (jnp.int32(0), least_tail(carry)) + carry)[2:]
    for h, cols in enumerate(heads):
        o_ref[:, cols] = carry[2 * h + 1].astype(BF16)


def _stick_breaking(qkv, batch, seq, tq, nh):
    nq = seq // tq
    ng = N_HEADS_C // nh
    w = nh * HEAD_DIM
    qmap = lambda b, g, i: (b * nq + i, g)
    return pl.pallas_call(
        functools.partial(_sb_kernel, tq=tq, nh=nh),
        grid=(batch, ng, nq),
        in_specs=[pl.BlockSpec((tq, w), qmap),
                  pl.BlockSpec((seq, w), lambda b, g, i: (b, ng + g)),
                  pl.BlockSpec((seq, w), lambda b, g, i: (b, 2 * ng + g))],
        out_specs=pl.BlockSpec((tq, w), qmap),
        out_shape=jax.ShapeDtypeStruct((batch * seq, N_HEADS_C * HEAD_DIM), BF16),
        compiler_params=_cparams(("parallel", "parallel", "parallel"), 32),
        name="stick_breaking",
    )(qkv, qkv, qkv)


def _xattn_kernel(xb_ref, xf_ref, kv_ref, wq_ref, wo_ref, g_ref, b_ref, of_ref):
    hk = N_HEADS_X * HEAD_DIM
    heads = [slice(h * HEAD_DIM, (h + 1) * HEAD_DIM) for h in range(N_HEADS_X)]
    for r in range(0, of_ref.shape[0], LN_SUB_ROWS):
        rows = slice(r, r + LN_SUB_ROWS)
        q = _dot(xb_ref[rows, :], wq_ref[...]).astype(BF16)
        logits = [_dot_nt(q[:, cols], kv_ref[:, cols]) * SCALE for cols in heads]
        probs, sums = [], []
        for lg in logits:
            e = jnp.exp(lg - jnp.max(lg, axis=1, keepdims=True))
            sums.append(jnp.sum(e, axis=1, keepdims=True))
            probs.append(e.astype(BF16))
        outs = [_dot(e, kv_ref[:, hk + h * HEAD_DIM:hk + (h + 1) * HEAD_DIM]) for h, e in enumerate(probs)]
        o = jnp.concatenate([(oh / s).astype(BF16) for oh, s in zip(outs, sums)], axis=1)
        of_ref[rows, :] = _layer_norm(ALPHA * xf_ref[rows, :] + _dot(o, wo_ref[...]), g_ref[...], b_ref[...])


def _xattn(xb, xf, kv, wq, wo, g, b, seq, tm):
    m, n = xf.shape
    per_seq = seq // tm
    row = lambda i: (i, 0)
    const = lambda i: (0, 0)
    return pl.pallas_call(
        _xattn_kernel,
        grid=(m // tm,),
        in_specs=[pl.BlockSpec((tm, n), row), pl.BlockSpec((tm, n), row),
                  pl.BlockSpec((MEM_LEN, kv.shape[1]), lambda i: (i // per_seq, 0)),
                  pl.BlockSpec(wq.shape, const), pl.BlockSpec(wo.shape, const),
                  pl.BlockSpec((1, n), const), pl.BlockSpec((1, n), const)],
        out_specs=pl.BlockSpec((tm, n), row),
        out_shape=jax.ShapeDtypeStruct((m, n), F32),
        compiler_params=_cparams(("parallel",), 56),
        name="xattn_ln",
    )(xb, xf, kv, wq, wo, g, b)


def _rope_tables(seq):
    pos = np.arange(seq, dtype=np.float64)[:, None]

    def half_tables(half):
        inv_freq = ROPE_THETA ** (-np.arange(half, dtype=np.float64) / half)
        ang = pos * inv_freq[None, :]
        return np.cos(ang).astype(np.float32), np.sin(ang).astype(np.float32)

    c64, s64 = half_tables(HEAD_DIM // 2)
    c32, s32 = half_tables(IDX_DIM // 2)
    rest = LANE - IDX_DIM
    ones, zeros = np.ones((seq, rest), np.float32), np.zeros((seq, rest), np.float32)
    head = (np.concatenate([c64, c64], 1), np.concatenate([-s64, s64], 1))
    idx = (np.concatenate([c32, c32, c32, c32], 1), np.concatenate([-s32, s32, -s32, s32], 1))
    idx_key = (np.concatenate([c32, c32, ones], 1), np.concatenate([-s32, s32, zeros], 1))
    return tuple(tuple(jnp.asarray(t) for t in pair) for pair in (head, idx, idx_key))


def kernel(x, mem, ln_g, ln_b, ffn_in, ffn_out, xattn_q, xattn_kv, xattn_o,
           even_in, even_out, even_rel_bias, odd_in, odd_out):
    batch, seq, d = x.shape
    t = batch * seq
    xf = x.reshape(t, d)
    memb = mem.reshape(batch * MEM_LEN, d).astype(BF16)
    tab_head, tab_idx, tab_idx_key = _rope_tables(seq)

    modes_a = ((0, HEAD_DIM),) * (N_HEADS_A + 1) + (None,)
    modes_i = ((0, IDX_DIM),) * ((IDX_HEADS * IDX_DIM) // LANE) + ((1, IDX_DIM),)

    def ln_params(layer, j):
        return ln_g[layer, j][None, :], ln_b[layer, j][None, :]

    def ffn(xf, layer, j):
        g, b = ln_params(layer, 3 * j)
        outs = _ffn(xf, ffn_in, ffn_out, (layer, j), g, b, 1024, 256, 0.5, want_bf16=(j == 0))
        return outs[0], (outs[1] if j == 0 else None)

    for layer in range(DEPTH):
        xf, xb = ffn(xf, layer, 0)

        g, b = ln_params(layer, 1)
        if layer % 2 == 0:
            w_in = even_in[layer // 2]
            w_a = w_in[:, :A_COLS].astype(BF16)
            w_i = jnp.pad(w_in[:, A_COLS:A_COLS + I_COLS], ((0, 0), (0, I_PAD - I_COLS))).astype(BF16)
            w_b = w_in[:, A_COLS + I_COLS:].astype(BF16)
            qkv_a = _proj_rope(xb, w_a, [tab_head], modes_a, seq, 512, False, "proj_dsa")[0]
            qk_i, w_idx = _proj_rope(xb, w_i, [tab_idx, tab_idx_key], modes_i, seq, 512, True, "proj_idx")
            qkv_b = _matmul(xb, w_b, 2048, 1024, "proj_band")
            o_a = _dsa_attention(qkv_a, qk_i, w_idx, batch, seq)
            o_b = _band_attention(qkv_b, _band_bias(even_rel_bias[layer // 2]), batch, seq)
            xf, xb = _out_ln([o_a, o_b], even_out[layer // 2].astype(BF16), xf, g, b, 512, "even_out_ln")
        else:
            qkv = _matmul(xb, odd_in, 1024, 1024, "proj_odd", lead=(layer // 2,))
            o = _stick_breaking(qkv, batch, seq, 256, 4)
            xf, xb = _out_ln([o], odd_out[layer // 2].astype(BF16), xf, g, b, 512, "odd_out_ln")

        g, b = ln_params(layer, 2)
        kv = _matmul(memb, xattn_kv[layer].astype(BF16), 1024, 1024, "proj_mem")
        xf = _xattn(xb, xf, kv, xattn_q[layer].astype(BF16), xattn_o[layer].astype(BF16), g, b, seq, 512)

        xf, _ = ffn(xf, layer, 1)

    return xf.reshape(batch, seq, d)
```

```python
import functools

import jax
import jax.numpy as jnp
import numpy as np
from jax import lax
from jax.experimental import pallas as pl
from jax.experimental.pallas import tpu as pltpu

F32 = jnp.float32
BF16 = jnp.bfloat16

D_MODEL = 2048
DEPTH = 4
CHUNK = 64
CHUNK_BITS = CHUNK.bit_length() - 1
MEM_LEN = 256
HEAD_DIM = 128
ROPE_THETA = 10000.0
LN_EPS = 1e-5
NEG_INF = -1e30
N_HEADS_A = 8
N_HEADS_B = 8
IDX_HEADS = 16
IDX_DIM = 64
TOPK_MAX = 256
LEFT_CHUNKS = 8
REL_CLIP = 128
REL_SIZE = CHUNK + REL_CLIP
N_HEADS_C = 16
N_HEADS_X = 4
FFN_DIM = ((8 * D_MODEL // 3 + 255) // 256) * 256
ALPHA = (2.0 * DEPTH) ** 0.25
SCALE = HEAD_DIM ** -0.5
LOG2_E = 1.4426950408889634

LANE = 128
INT_MIN = -2147483648
MIB = 1024 * 1024

A_COLS = (N_HEADS_A + 2) * HEAD_DIM
I_COLS = IDX_HEADS * IDX_DIM + IDX_DIM + IDX_HEADS
I_PAD = ((I_COLS + LANE - 1) // LANE) * LANE
W_IDX_LANE = IDX_DIM

BAND_Q = 2 * CHUNK
BAND_KB = (LEFT_CHUNKS * CHUNK) // BAND_Q + 1
BAND_W = BAND_KB * BAND_Q

LN_SUB_ROWS = 256

SB_DEAD_BITS = 152.0

DSA_KEY_SPAN = 256
DSA_HEAD_GROUP = 4
BIT_GROUP = 32 * 8


def _cparams(sem, vmem_mib):
    return pltpu.CompilerParams(dimension_semantics=sem, vmem_limit_bytes=vmem_mib * MIB)


def _dot(a, b):
    return jnp.dot(a, b, preferred_element_type=F32)


def _dot_nt(a, b):
    return lax.dot_general(a, b, (((1,), (1,)), ((), ())), preferred_element_type=F32)


def _layer_norm(y, g, b):
    mu = jnp.mean(y, axis=-1, keepdims=True)
    d = y - mu
    var = jnp.mean(d * d, axis=-1, keepdims=True)
    return d * lax.rsqrt(var + LN_EPS) * g + b


def _mm_kernel(x_ref, w_ref, o_ref, *scratch):
    if scratch:
        wb_ref, = scratch

        @pl.when(pl.program_id(1) == 0)
        def _():
            wb_ref[...] = w_ref[...].astype(BF16)
    else:
        wb_ref = w_ref
    o_ref[...] = _dot(x_ref[...], wb_ref[...]).astype(o_ref.dtype)


def _matmul(x, w, tm, tn, name, lead=()):
    m, k = x.shape
    n = w.shape[-1]
    cast = w.dtype != BF16
    return pl.pallas_call(
        _mm_kernel,
        grid=(n // tn, m // tm),
        in_specs=[pl.BlockSpec((tm, k), lambda j, i: (i, 0)),
                  pl.BlockSpec((None,) * len(lead) + (k, tn), lambda j, i: lead + (0, j))],
        out_specs=pl.BlockSpec((tm, tn), lambda j, i: (i, j)),
        out_shape=jax.ShapeDtypeStruct((m, n), BF16),
        scratch_shapes=[pltpu.VMEM((k, tn), BF16)] if cast else [],
        compiler_params=_cparams(("parallel", "arbitrary"), 48),
        name=name,
    )(x, w)


def _rope_group(x, cos, sin, width):
    if width == LANE:
        partner = pltpu.roll(x, LANE // 2, 1)
    else:
        lane = lax.broadcasted_iota(jnp.int32, x.shape, 1)
        half = width // 2
        partner = jnp.where((lane & half) == 0, pltpu.roll(x, LANE - half, 1), pltpu.roll(x, half, 1))
    return x * cos + partner * sin


def _proj_rope_kernel(x_ref, w_ref, *rest, modes, n_tab, f32_tail):
    tabs = rest[:2 * n_tab]
    outs = rest[2 * n_tab:]
    acc = _dot(x_ref[...], w_ref[...])
    for g, mode in enumerate(modes):
        blk = acc[:, g * LANE:(g + 1) * LANE]
        if mode is not None:
            tid, width = mode
            blk = _rope_group(blk, tabs[2 * tid][...], tabs[2 * tid + 1][...], width)
        outs[0][:, g * LANE:(g + 1) * LANE] = blk.astype(BF16)
    if f32_tail:
        outs[1][...] = acc[:, -LANE:]


def _proj_rope(x, w, tabs, modes, seq, tm, f32_tail, name):
    m, k = x.shape
    n = w.shape[1]
    per_seq = seq // tm
    in_specs = [pl.BlockSpec((tm, k), lambda i: (i, 0)), pl.BlockSpec((k, n), lambda i: (0, 0))]
    args = [x, w]
    for cos, sin in tabs:
        in_specs += [pl.BlockSpec((tm, LANE), lambda i: (i % per_seq, 0))] * 2
        args += [cos, sin]
    out_specs = [pl.BlockSpec((tm, n), lambda i: (i, 0))]
    out_shape = [jax.ShapeDtypeStruct((m, n), BF16)]
    if f32_tail:
        out_specs.append(pl.BlockSpec((tm, LANE), lambda i: (i, 0)))
        out_shape.append(jax.ShapeDtypeStruct((m, LANE), F32))
    return pl.pallas_call(
        functools.partial(_proj_rope_kernel, modes=modes, n_tab=len(tabs), f32_tail=f32_tail),
        grid=(m // tm,),
        in_specs=in_specs,
        out_specs=out_specs,
        out_shape=out_shape,
        compiler_params=_cparams(("parallel",), 48),
        name=name,
    )(*args)


def _ffn_kernel(x_ref, wa_ref, wg_ref, wo_ref, g_ref, b_ref, *rest, nf, tf, scale, want_bf16):
    if want_bf16:
        of_ref, ob_ref = rest
        xb_ref = ob_ref
    else:
        of_ref, xb_ref = rest
    j = pl.program_id(1)

    @pl.when(j == 0)
    def _():
        xb_ref[...] = x_ref[...].astype(BF16)

    subs = [slice(r, r + LN_SUB_ROWS) for r in range(0, of_ref.shape[0], LN_SUB_ROWS)]

    def hidden_slice(phase):
        w_in = jnp.concatenate([wa_ref[...], wg_ref[...]], axis=1).astype(BF16)
        wo = wo_ref[...].astype(BF16)
        ags = [_dot(xb_ref[rows, :], w_in) for rows in subs]
        hs = [(ag[:, :tf] * jax.nn.sigmoid(ag[:, :tf]) * ag[:, tf:]).astype(BF16) for ag in ags]
        parts = [_dot(h, wo) for h in hs]
        for rows, part in zip(subs, parts):
            if phase == "first":
                of_ref[rows, :] = part
            elif phase == "middle":
                of_ref[rows, :] += part
            else:
                y = _layer_norm(ALPHA * x_ref[rows, :] + scale * (of_ref[rows, :] + part), g_ref[...], b_ref[...])
                of_ref[rows, :] = y
                if want_bf16:
                    ob_ref[rows, :] = y.astype(BF16)

    pl.when(j == 0)(functools.partial(hidden_slice, "first"))
    pl.when(jnp.logical_and(j > 0, j < nf - 1))(functools.partial(hidden_slice, "middle"))
    pl.when(j == nf - 1)(functools.partial(hidden_slice, "last"))


def _ffn(xf, w_in, w_out, lead, g, b, tm, tf, scale, want_bf16):
    m, d = xf.shape
    nf = FFN_DIM // tf
    nl = (None,) * len(lead)
    row = lambda i, j: (i, 0)
    const = lambda i, j: (0, 0)
    out_specs = [pl.BlockSpec((tm, d), row)]
    out_shape = [jax.ShapeDtypeStruct((m, d), F32)]
    if want_bf16:
        out_specs.append(pl.BlockSpec((tm, d), row))
        out_shape.append(jax.ShapeDtypeStruct((m, d), BF16))
    return pl.pallas_call(
        functools.partial(_ffn_kernel, nf=nf, tf=tf, scale=scale, want_bf16=want_bf16),
        grid=(m // tm, nf),
        in_specs=[pl.BlockSpec((tm, d), row),
                  pl.BlockSpec(nl + (d, tf), lambda i, j: lead + (0, j)),
                  pl.BlockSpec(nl + (d, tf), lambda i, j: lead + (0, j + nf)),
                  pl.BlockSpec(nl + (tf, d), lambda i, j: lead + (j, 0)),
                  pl.BlockSpec((1, d), const), pl.BlockSpec((1, d), const)],
        out_specs=out_specs,
        out_shape=out_shape,
        scratch_shapes=[] if want_bf16 else [pltpu.VMEM((tm, d), BF16)],
        compiler_params=_cparams(("parallel", "arbitrary"), 60),
        name="ffn_fused",
    )(xf, w_in, w_in, w_out, g, b)


def _out_ln_kernel(*refs, n_lhs):
    lhs = refs[:n_lhs]
    ws = refs[n_lhs:2 * n_lhs]
    x_ref, g_ref, b_ref, of_ref, ob_ref = refs[2 * n_lhs:]
    for r in range(0, of_ref.shape[0], LN_SUB_ROWS):
        rows = slice(r, r + LN_SUB_ROWS)
        acc = _dot(lhs[0][rows, :], ws[0][...])
        for l_ref, w_ref in zip(lhs[1:], ws[1:]):
            acc = acc + _dot(l_ref[rows, :], w_ref[...])
        y = _layer_norm(ALPHA * x_ref[rows, :] + acc, g_ref[...], b_ref[...])
        of_ref[rows, :] = y
        ob_ref[rows, :] = y.astype(BF16)


def _out_ln(lhs_list, w, xf, g, b, tm, name):
    m, n = xf.shape
    row = lambda i: (i, 0)
    in_specs, args, off = [], [], 0
    for l in lhs_list:
        in_specs.append(pl.BlockSpec((tm, l.shape[1]), row))
        args.append(l)
    for l in lhs_list:
        kl = l.shape[1]
        in_specs.append(pl.BlockSpec((kl, n), functools.partial(lambda i, o: (o, 0), o=off // kl)))
        args.append(w)
        off += kl
    in_specs += [pl.BlockSpec((tm, n), row), pl.BlockSpec((1, n), lambda i: (0, 0)), pl.BlockSpec((1, n), lambda i: (0, 0))]
    args += [xf, g, b]
    return pl.pallas_call(
        functools.partial(_out_ln_kernel, n_lhs=len(lhs_list)),
        grid=(m // tm,),
        in_specs=in_specs,
        out_specs=[pl.BlockSpec((tm, n), row), pl.BlockSpec((tm, n), row)],
        out_shape=[jax.ShapeDtypeStruct((m, n), F32), jax.ShapeDtypeStruct((m, n), BF16)],
        compiler_params=_cparams(("parallel",), 56),
        name=name,
    )(*args)


def _dsa_kernel(q_ref, k_ref, v_ref, qi_ref, ki_ref, w_ref, o_ref, plane_ref, bias_ref, *, tq, seq, topk):
    i = pl.program_id(1)
    per_span = DSA_KEY_SPAN // tq
    for n in range(1, seq // DSA_KEY_SPAN + 1):
        pl.when(i // per_span == n - 1)(functools.partial(
            _dsa_body, i, q_ref, k_ref, v_ref, qi_ref, ki_ref, w_ref, o_ref, plane_ref, bias_ref,
            tq=tq, nk=n * DSA_KEY_SPAN, topk=topk))


def _dsa_body(i, q_ref, k_ref, v_ref, qi_ref, ki_ref, w_ref, o_ref, plane_ref, bias_ref, *, tq, nk, topk):
    span = DSA_KEY_SPAN
    n_groups = nk // BIT_GROUP
    w_t = w_ref[...].T
    key_pos = nk - span + lax.broadcasted_iota(jnp.int32, (span, tq), 0)
    q_pos = i * tq + lax.broadcasted_iota(jnp.int32, (span, tq), 1)
    allowed = (key_pos >> CHUNK_BITS) <= (q_pos >> CHUNK_BITS)
    qi_all = jnp.concatenate([qi_ref[:, h * IDX_DIM:(h + 1) * IDX_DIM] for h in range(IDX_HEADS)], axis=0)
    for c in range(nk // span):
        rel = jnp.maximum(_dot_nt(ki_ref[c * span:(c + 1) * span, :IDX_DIM], qi_all), 0.0)
        score = jnp.zeros((span, tq), F32)
        for h in range(IDX_HEADS):
            score = score + rel[:, h * tq:(h + 1) * tq] * w_t[W_IDX_LANE + h:W_IDX_LANE + h + 1, :]
        if c == nk // span - 1:
            score = jnp.where(allowed, score, NEG_INF)
        bits = lax.bitcast_convert_type(score, jnp.int32)
        u = bits ^ ((bits >> 31) | INT_MIN)
        for gg in range(span // BIT_GROUP):
            words = [u[gg * BIT_GROUP + 8 * j:gg * BIT_GROUP + 8 * j + 8, :] for j in range(32)]
            g = c * (span // BIT_GROUP) + gg
            for b, plane in enumerate(_bit_transpose32(words)):
                plane_ref[g, b] = plane

    def select_bit(j, state):
        alive, keep, above = state[:n_groups], state[n_groups:2 * n_groups], state[-1]
        ones = [a & plane_ref[g, 31 - j] for g, a in enumerate(alive)]
        cnt = lax.population_count(ones[0])
        for o in ones[1:]:
            cnt = cnt + lax.population_count(o)
        for shift in (4, 2, 1):
            cnt = cnt + pltpu.roll(cnt, shift, 0)
        take = (above + cnt) >= topk
        zeros = [a ^ o for a, o in zip(alive, ones)]
        new_alive = [jnp.where(take, o, z) for o, z in zip(ones, zeros)]
        new_keep = [jnp.where(take, kp ^ z, kp) for kp, z in zip(keep, zeros)]
        return tuple(new_alive) + tuple(new_keep) + (jnp.where(take, above, above + cnt),)

    everything = jnp.full((8, tq), -1, jnp.int32)
    state = lax.fori_loop(0, 32, select_bit, (everything,) * (2 * n_groups) + (jnp.zeros((8, tq), jnp.int32),))
    alive, keep, above = state[:n_groups], state[n_groups:2 * n_groups], state[-1]

    need = topk - above
    sub = lax.broadcasted_iota(jnp.int32, (8, tq), 0)

    def below(limit, g):
        n_bits = jnp.clip((limit - g * BIT_GROUP - sub + 7) >> 3, 0, 32)
        return jnp.where(n_bits >= 32, -1, lax.shift_left(jnp.int32(1), jnp.minimum(n_bits, 31)) - 1)

    bound = jnp.zeros((8, tq), jnp.int32)
    step = 1 << ((nk - 1).bit_length() - 1)
    while step >= 1:
        cnt = lax.population_count(alive[0] & below(bound + step, 0))
        for g in range(1, n_groups):
            cnt = cnt + lax.population_count(alive[g] & below(bound + step, g))
        for shift in (4, 2, 1):
            cnt = cnt + pltpu.roll(cnt, shift, 0)
        bound = jnp.where(cnt < need, bound + step, bound)
        step //= 2
    keep = [kp ^ (a & ~below(bound + 1, g)) for g, (kp, a) in enumerate(zip(keep, alive))]

    for g in range(n_groups):
        for j in range(32):
            rows = slice(g * BIT_GROUP + 8 * j, g * BIT_GROUP + 8 * j + 8)
            bias = jnp.where(lax.shift_left(keep[g], 31 - j) < 0, 0.0, NEG_INF)
            if g * BIT_GROUP + 8 * j >= nk - span:
                bias = jnp.where(allowed[rows.start - (nk - span):rows.stop - (nk - span), :], bias, NEG_INF)
            bias_ref[rows, :] = bias

    k = k_ref[:nk, :]
    v_ext = jnp.concatenate([v_ref[:nk, :], jnp.ones((nk, HEAD_DIM), BF16)], axis=1)
    groups = [[slice(h * HEAD_DIM, (h + 1) * HEAD_DIM) for h in range(g * DSA_HEAD_GROUP, (g + 1) * DSA_HEAD_GROUP)]
              for g in range(N_HEADS_A // DSA_HEAD_GROUP)]
    logits = [_dot_nt(k, jnp.concatenate([q_ref[:, cols] for cols in heads], axis=0)) * SCALE for heads in groups]
    probs = []
    for lg in logits:
        lg = jnp.concatenate([lg[:, j * tq:(j + 1) * tq] + bias_ref[:nk, :] for j in range(DSA_HEAD_GROUP)], axis=1)
        probs.append(jnp.exp(lg - jnp.max(lg, axis=0, keepdims=True)).astype(BF16))
    outs = [lax.dot_general(e, v_ext, (((0,), (0,)), ((), ())), preferred_element_type=F32) for e in probs]
    for heads, o in zip(groups, outs):
        o = o[:, :HEAD_DIM] / o[:, HEAD_DIM:HEAD_DIM + 1]
        for j, cols in enumerate(heads):
            o_ref[:, cols] = o[j * tq:(j + 1) * tq].astype(BF16)


def _bit_transpose32(words):
    words = list(words)
    for dist, low in ((16, 0x0000FFFF), (8, 0x00FF00FF), (4, 0x0F0F0F0F), (2, 0x33333333), (1, 0x55555555)):
        for k in range(32):
            if k & dist:
                continue
            a, b = words[k], words[k + dist]
            t = (lax.shift_right_logical(a, dist) ^ b) & low
            words[k + dist] = b ^ t
            words[k] = a ^ lax.shift_left(t, dist)
    return words


def _dsa_attention(qkv_a, qk_i, w_i, batch, seq):
    tq = 2 * CHUNK
    nq = seq // tq
    hq = N_HEADS_A * HEAD_DIM
    kcol = hq // LANE
    icol = (IDX_HEADS * IDX_DIM) // LANE
    qmap = lambda b, i: (b * nq + i, 0)
    return pl.pallas_call(
        functools.partial(_dsa_kernel, tq=tq, seq=seq, topk=min(TOPK_MAX, seq // 4)),
        grid=(batch, nq),
        in_specs=[pl.BlockSpec((tq, hq), qmap),
                  pl.BlockSpec((seq, LANE), lambda b, i: (b, kcol)),
                  pl.BlockSpec((seq, LANE), lambda b, i: (b, kcol + 1)),
                  pl.BlockSpec((tq, IDX_HEADS * IDX_DIM), qmap),
                  pl.BlockSpec((seq, LANE), lambda b, i: (b, icol)),
                  pl.BlockSpec((tq, LANE), qmap)],
        out_specs=pl.BlockSpec((tq, hq), qmap),
        out_shape=jax.ShapeDtypeStruct((batch * seq, hq), BF16),
        scratch_shapes=[pltpu.VMEM((seq // BIT_GROUP, 32, 8, tq), jnp.int32), pltpu.VMEM((seq, tq), F32)],
        compiler_params=_cparams(("parallel", "parallel"), 48),
        name="dsa_attention",
    )(qkv_a, qkv_a, qkv_a, qk_i, qk_i, w_i)


def _band_bias_kernel(rel_ref, o_ref):
    h = pl.program_id(0)
    i = lax.broadcasted_iota(jnp.int32, (BAND_Q, BAND_Q), 0)
    jj = lax.broadcasted_iota(jnp.int32, (BAND_Q, BAND_Q), 1)
    for jb in range(BAND_KB):
        j = jb * BAND_Q + jj
        idx = jnp.clip(i - j + LEFT_CHUNKS * CHUNK, -(CHUNK - 1), REL_CLIP) + (CHUNK - 1)
        cq = i >> CHUNK_BITS
        ck = j >> CHUNK_BITS
        d_lo = LEFT_CHUNKS * CHUNK - (jb + 1) * BAND_Q + 1
        d_hi = LEFT_CHUNKS * CHUNK - jb * BAND_Q + BAND_Q - 1
        r_lo = min(max(d_lo, -(CHUNK - 1)), REL_CLIP) + (CHUNK - 1)
        r_hi = min(max(d_hi, -(CHUNK - 1)), REL_CLIP) + (CHUNK - 1)
        val = lax.fori_loop(r_lo, r_hi + 1, lambda r, acc: jnp.where(idx == r, rel_ref[h, r], acc),
                            jnp.zeros((BAND_Q, BAND_Q), F32))
        val = jnp.where(ck >= cq, val, NEG_INF)
        o_ref[0, jb] = jnp.where(ck <= cq + LEFT_CHUNKS, val, NEG_INF)


def _band_bias(rel_bias):
    nh = rel_bias.shape[0]
    return pl.pallas_call(
        _band_bias_kernel,
        grid=(nh,),
        in_specs=[pl.BlockSpec(memory_space=pltpu.SMEM)],
        out_specs=pl.BlockSpec((1, BAND_KB, BAND_Q, BAND_Q), lambda h: (h, 0, 0, 0)),
        out_shape=jax.ShapeDtypeStruct((nh, BAND_KB, BAND_Q, BAND_Q), F32),
        name="band_bias",
    )(rel_bias)


def _band_kernel(q_ref, k_ref, v_ref, bias_ref, o_ref):
    p = pl.program_id(1)
    first = BAND_KB - 1
    shift = jnp.maximum(first - p, 0)
    s0 = pl.multiple_of(jnp.maximum(p - first, 0) * BAND_Q, BAND_Q)
    heads = [slice(h * HEAD_DIM, (h + 1) * HEAD_DIM) for h in range(N_HEADS_B)]
    logits = [_dot_nt(q_ref[:, cols], k_ref[pl.ds(s0, BAND_W), cols]) * SCALE for cols in heads]
    probs, sums = [], []
    for h, lg in enumerate(logits):
        blocks = []
        for jb in range(BAND_KB):
            src = jb + shift
            blk = lg[:, jb * BAND_Q:(jb + 1) * BAND_Q] + bias_ref[h, jnp.minimum(src, first)]
            blocks.append(jnp.where(src <= first, blk, NEG_INF))
        lg = jnp.concatenate(blocks, axis=1)
        e = jnp.exp(lg - jnp.max(lg, axis=1, keepdims=True))
        sums.append(jnp.sum(e, axis=1, keepdims=True))
        probs.append(e.astype(BF16))
    outs = [_dot(e, v_ref[pl.ds(s0, BAND_W), cols]) for e, cols in zip(probs, heads)]
    for o, s, cols in zip(outs, sums, heads):
        o_ref[:, cols] = (o / s).astype(BF16)


def _band_attention(qkv_b, bias, batch, seq):
    nq = seq // BAND_Q
    hq = N_HEADS_B * HEAD_DIM
    qmap = lambda b, p: (b * nq + p, 0)
    return pl.pallas_call(
        _band_kernel,
        grid=(batch, nq),
        in_specs=[pl.BlockSpec((BAND_Q, hq), qmap),
                  pl.BlockSpec((seq, hq), lambda b, p: (b, 1)),
                  pl.BlockSpec((seq, hq), lambda b, p: (b, 2)),
                  pl.BlockSpec(bias.shape, lambda b, p: (0, 0, 0, 0))],
        out_specs=pl.BlockSpec((BAND_Q, hq), qmap),
        out_shape=jax.ShapeDtypeStruct((batch * seq, hq), BF16),
        compiler_params=_cparams(("parallel", "parallel"), 48),
        name="band_attention",
    )(qkv_b, qkv_b, qkv_b, bias)


def _sb_blocks(qs, k_blks, v_blks, later, carry, past):
    zs = [_dot_nt(q, k) * (SCALE * LOG2_E) for q, k in zip(qs, k_blks)]
    log_betas, drops, splits = [], [], []
    for z2 in zs:
        neg_abs = lax.bitcast_convert_type(lax.bitcast_convert_type(z2, jnp.int32) | INT_MIN, F32)
        drop = jnp.maximum(z2, 0.0) + jnp.log2(1.0 + jnp.exp2(neg_abs))
        log_betas.append(z2 - drop)
        if past is not None:
            drop = jnp.where(past, drop, 0.0)
        drops.append(drop)
        hi = drop.astype(BF16)
        r1 = drop - hi.astype(F32)
        mid = r1.astype(BF16)
        lo = (r1 - mid.astype(F32)).astype(BF16)
        splits.append(jnp.concatenate([hi, mid, lo], axis=1))
    afters = [_dot(s, later) for s in splits]
    probs, tails = [], []
    for h, (log_beta, drop, after) in enumerate(zip(log_betas, drops, afters)):
        row_sum = jnp.sum(drop, axis=1, keepdims=True)
        if carry is None:
            a = jnp.exp2(log_beta - after)
            tails.append(row_sum)
        else:
            a = jnp.exp2(log_beta - carry[2 * h] - after)
            tails.append(carry[2 * h] + row_sum)
        if past is not None:
            a = jnp.where(past, a, 0.0)
        probs.append(a.astype(BF16))
    pvs = [_dot(a, v) for a, v in zip(probs, v_blks)]
    out = []
    for h, (tail, pv) in enumerate(zip(tails, pvs)):
        out += [tail, pv if carry is None else carry[2 * h + 1] + pv]
    return tuple(out)


def _sb_kernel(q_ref, k_ref, v_ref, o_ref, *, tq, nh):
    i = pl.program_id(2)
    r3 = lax.broadcasted_iota(jnp.int32, (3 * tq, tq), 0)
    c3 = lax.broadcasted_iota(jnp.int32, (3 * tq, tq), 1)
    later = jnp.where((r3 & (tq - 1)) > c3, 1.0, 0.0).astype(BF16)
    row = lax.broadcasted_iota(jnp.int32, (tq, tq), 0)
    col = lax.broadcasted_iota(jnp.int32, (tq, tq), 1)
    heads = [slice(h * HEAD_DIM, (h + 1) * HEAD_DIM) for h in range(nh)]

    def walk(s0, carry, past):
        return _sb_blocks([q_ref[:, cols] for cols in heads],
                          [k_ref[pl.ds(s0, tq), cols] for cols in heads],
                          [v_ref[pl.ds(s0, tq), cols] for cols in heads], later, carry, past)

    def least_tail(c):
        m = c[0]
        for h in range(1, nh):
            m = jnp.minimum(m, c[2 * h])
        return jnp.min(m)

    def alive(state):
        return jnp.logical_and(state[0] < i, state[1] < SB_DEAD_BITS)

    def step(state):
        c = walk(pl.multiple_of((i - 1 - state[0]) * tq, tq), state[2:], None)
        return (state[0] + 1, least_tail(c)) + c

    carry = walk(pl.multiple_of(i * tq, tq), None, col < row)
    carry = lax.while_loop(alive, step, (jnp.int32(0), least_tail(carry)) + carry)[2:]
    for h, cols in enumerate(heads):
        o_ref[:, cols] = carry[2 * h + 1].astype(BF16)


def _stick_breaking(qkv, batch, seq, tq, nh):
    nq = seq // tq
    ng = N_HEADS_C // nh
    w = nh * HEAD_DIM
    qmap = lambda b, g, i: (b * nq + i, g)
    return pl.pallas_call(
        functools.partial(_sb_kernel, tq=tq, nh=nh),
        grid=(batch, ng, nq),
        in_specs=[pl.BlockSpec((tq, w), qmap),
                  pl.BlockSpec((seq, w), lambda b, g, i: (b, ng + g)),
                  pl.BlockSpec((seq, w), lambda b, g, i: (b, 2 * ng + g))],
        out_specs=pl.BlockSpec((tq, w), qmap),
        out_shape=jax.ShapeDtypeStruct((batch * seq, N_HEADS_C * HEAD_DIM), BF16),
        compiler_params=_cparams(("parallel", "parallel", "parallel"), 32),
        name="stick_breaking",
    )(qkv, qkv, qkv)


def _xattn_kernel(xb_ref, xf_ref, kv_ref, wq_ref, wo_ref, g_ref, b_ref, of_ref, wqb_ref, wob_ref):
    @pl.when(pl.program_id(0) == 0)
    def _():
        wqb_ref[...] = wq_ref[...].astype(BF16)
        wob_ref[...] = wo_ref[...].astype(BF16)

    hk = N_HEADS_X * HEAD_DIM
    heads = [slice(h * HEAD_DIM, (h + 1) * HEAD_DIM) for h in range(N_HEADS_X)]
    for r in range(0, of_ref.shape[0], LN_SUB_ROWS):
        rows = slice(r, r + LN_SUB_ROWS)
        q = _dot(xb_ref[rows, :], wqb_ref[...]).astype(BF16)
        logits = [_dot_nt(q[:, cols], kv_ref[:, cols]) * SCALE for cols in heads]
        probs, sums = [], []
        for lg in logits:
            e = jnp.exp(lg - jnp.max(lg, axis=1, keepdims=True))
            sums.append(jnp.sum(e, axis=1, keepdims=True))
            probs.append(e.astype(BF16))
        outs = [_dot(e, kv_ref[:, hk + h * HEAD_DIM:hk + (h + 1) * HEAD_DIM]) for h, e in enumerate(probs)]
        o = jnp.concatenate([(oh / s).astype(BF16) for oh, s in zip(outs, sums)], axis=1)
        of_ref[rows, :] = _layer_norm(ALPHA * xf_ref[rows, :] + _dot(o, wob_ref[...]), g_ref[...], b_ref[...])


def _xattn(xb, xf, kv, wq, wo, lead, g, b, seq, tm):
    m, n = xf.shape
    per_seq = seq // tm
    nl = (None,) * len(lead)
    row = lambda i: (i, 0)
    const = lambda i: (0, 0)
    return pl.pallas_call(
        _xattn_kernel,
        grid=(m // tm,),
        in_specs=[pl.BlockSpec((tm, n), row), pl.BlockSpec((tm, n), row),
                  pl.BlockSpec((MEM_LEN, kv.shape[1]), lambda i: (i // per_seq, 0)),
                  pl.BlockSpec(nl + wq.shape[-2:], lambda i: lead + (0, 0)),
                  pl.BlockSpec(nl + wo.shape[-2:], lambda i: lead + (0, 0)),
                  pl.BlockSpec((1, n), const), pl.BlockSpec((1, n), const)],
        out_specs=pl.BlockSpec((tm, n), row),
        out_shape=jax.ShapeDtypeStruct((m, n), F32),
        scratch_shapes=[pltpu.VMEM(wq.shape[-2:], BF16), pltpu.VMEM(wo.shape[-2:], BF16)],
        compiler_params=_cparams(("arbitrary",), 56),
        name="xattn_ln",
    )(xb, xf, kv, wq, wo, g, b)


def _rope_tables(seq):
    pos = np.arange(seq, dtype=np.float64)[:, None]

    def half_tables(half):
        inv_freq = ROPE_THETA ** (-np.arange(half, dtype=np.float64) / half)
        ang = pos * inv_freq[None, :]
        return np.cos(ang).astype(np.float32), np.sin(ang).astype(np.float32)

    c64, s64 = half_tables(HEAD_DIM // 2)
    c32, s32 = half_tables(IDX_DIM // 2)
    rest = LANE - IDX_DIM
    ones, zeros = np.ones((seq, rest), np.float32), np.zeros((seq, rest), np.float32)
    head = (np.concatenate([c64, c64], 1), np.concatenate([-s64, s64], 1))
    idx = (np.concatenate([c32, c32, c32, c32], 1), np.concatenate([-s32, s32, -s32, s32], 1))
    idx_key = (np.concatenate([c32, c32, ones], 1), np.concatenate([-s32, s32, zeros], 1))
    return tuple(tuple(jnp.asarray(t) for t in pair) for pair in (head, idx, idx_key))


def kernel(x, mem, ln_g, ln_b, ffn_in, ffn_out, xattn_q, xattn_kv, xattn_o,
           even_in, even_out, even_rel_bias, odd_in, odd_out):
    batch, seq, d = x.shape
    t = batch * seq
    xf = x.reshape(t, d)
    memb = mem.reshape(batch * MEM_LEN, d).astype(BF16)
    tab_head, tab_idx, tab_idx_key = _rope_tables(seq)

    modes_a = ((0, HEAD_DIM),) * (N_HEADS_A + 1) + (None,)
    modes_i = ((0, IDX_DIM),) * ((IDX_HEADS * IDX_DIM) // LANE) + ((1, IDX_DIM),)

    def ln_params(layer, j):
        return ln_g[layer, j][None, :], ln_b[layer, j][None, :]

    def ffn(xf, layer, j):
        g, b = ln_params(layer, 3 * j)
        outs = _ffn(xf, ffn_in, ffn_out, (layer, j), g, b, 1024, 256, 0.5, want_bf16=(j == 0))
        return outs[0], (outs[1] if j == 0 else None)

    for layer in range(DEPTH):
        xf, xb = ffn(xf, layer, 0)

        g, b = ln_params(layer, 1)
        if layer % 2 == 0:
            w_in = even_in[layer // 2]
            w_a = w_in[:, :A_COLS].astype(BF16)
            w_i = jnp.pad(w_in[:, A_COLS:A_COLS + I_COLS], ((0, 0), (0, I_PAD - I_COLS))).astype(BF16)
            w_b = w_in[:, A_COLS + I_COLS:].astype(BF16)
            qkv_a = _proj_rope(xb, w_a, [tab_head], modes_a, seq, 512, False, "proj_dsa")[0]
            qk_i, w_idx = _proj_rope(xb, w_i, [tab_idx, tab_idx_key], modes_i, seq, 512, True, "proj_idx")
            qkv_b = _matmul(xb, w_b, 2048, 1024, "proj_band")
            o_a = _dsa_attention(qkv_a, qk_i, w_idx, batch, seq)
            o_b = _band_attention(qkv_b, _band_bias(even_rel_bias[layer // 2]), batch, seq)
            xf, xb = _out_ln([o_a, o_b], even_out[layer // 2].astype(BF16), xf, g, b, 512, "even_out_ln")
        else:
            qkv = _matmul(xb, odd_in, 1024, 1024, "proj_odd", lead=(layer // 2,))
            o = _stick_breaking(qkv, batch, seq, 256, 4)
            xf, xb = _out_ln([o], odd_out[layer // 2].astype(BF16), xf, g, b, 512, "odd_out_ln")

        g, b = ln_params(layer, 2)
        kv = _matmul(memb, xattn_kv, 1024, 1024, "proj_mem", lead=(layer,))
        xf = _xattn(xb, xf, kv, xattn_q, xattn_o, (layer,), g, b, seq, 512)

        xf, _ = ffn(xf, layer, 1)

    return xf.reshape(batch, seq, d)
```

```python
import functools

import jax
import jax.numpy as jnp
import numpy as np
from jax import lax
from jax.experimental import pallas as pl
from jax.experimental.pallas import tpu as pltpu

F32 = jnp.float32
BF16 = jnp.bfloat16

D_MODEL = 2048
DEPTH = 4
CHUNK = 64
CHUNK_BITS = CHUNK.bit_length() - 1
MEM_LEN = 256
HEAD_DIM = 128
ROPE_THETA = 10000.0
LN_EPS = 1e-5
NEG_INF = -1e30
N_HEADS_A = 8
N_HEADS_B = 8
IDX_HEADS = 16
IDX_DIM = 64
TOPK_MAX = 256
LEFT_CHUNKS = 8
REL_CLIP = 128
REL_SIZE = CHUNK + REL_CLIP
N_HEADS_C = 16
N_HEADS_X = 4
FFN_DIM = ((8 * D_MODEL // 3 + 255) // 256) * 256
ALPHA = (2.0 * DEPTH) ** 0.25
SCALE = HEAD_DIM ** -0.5
LOG2_E = 1.4426950408889634

LANE = 128
INT_MIN = -2147483648
MIB = 1024 * 1024

A_COLS = (N_HEADS_A + 2) * HEAD_DIM
I_COLS = IDX_HEADS * IDX_DIM + IDX_DIM + IDX_HEADS
I_PAD = ((I_COLS + LANE - 1) // LANE) * LANE
W_IDX_LANE = IDX_DIM

BAND_Q = 2 * CHUNK
BAND_KB = (LEFT_CHUNKS * CHUNK) // BAND_Q + 1
BAND_W = BAND_KB * BAND_Q

LN_SUB_ROWS = 256

SB_DEAD_BITS = 152.0

DSA_KEY_SPAN = 256
DSA_HEAD_GROUP = 4
BIT_GROUP = 32 * 8


def _cparams(sem, vmem_mib):
    return pltpu.CompilerParams(dimension_semantics=sem, vmem_limit_bytes=vmem_mib * MIB)


def _dot(a, b):
    return jnp.dot(a, b, preferred_element_type=F32)


def _dot_nt(a, b):
    return lax.dot_general(a, b, (((1,), (1,)), ((), ())), preferred_element_type=F32)


def _layer_norm(y, g, b):
    mu = jnp.mean(y, axis=-1, keepdims=True)
    d = y - mu
    var = jnp.mean(d * d, axis=-1, keepdims=True)
    return d * lax.rsqrt(var + LN_EPS) * g + b


def _mm_kernel(x_ref, w_ref, o_ref, *scratch):
    if scratch:
        wb_ref, = scratch

        @pl.when(pl.program_id(1) == 0)
        def _():
            wb_ref[...] = w_ref[...].astype(BF16)
    else:
        wb_ref = w_ref
    o_ref[...] = _dot(x_ref[...], wb_ref[...]).astype(o_ref.dtype)


def _matmul(x, w, tm, tn, name, lead=()):
    m, k = x.shape
    n = w.shape[-1]
    cast = w.dtype != BF16
    return pl.pallas_call(
        _mm_kernel,
        grid=(n // tn, m // tm),
        in_specs=[pl.BlockSpec((tm, k), lambda j, i: (i, 0)),
                  pl.BlockSpec((None,) * len(lead) + (k, tn), lambda j, i: lead + (0, j))],
        out_specs=pl.BlockSpec((tm, tn), lambda j, i: (i, j)),
        out_shape=jax.ShapeDtypeStruct((m, n), BF16),
        scratch_shapes=[pltpu.VMEM((k, tn), BF16)] if cast else [],
        compiler_params=_cparams(("parallel", "arbitrary"), 48),
        name=name,
    )(x, w)


def _rope_group(x, cos, sin, width):
    if width == LANE:
        partner = pltpu.roll(x, LANE // 2, 1)
    else:
        lane = lax.broadcasted_iota(jnp.int32, x.shape, 1)
        half = width // 2
        partner = jnp.where((lane & half) == 0, pltpu.roll(x, LANE - half, 1), pltpu.roll(x, half, 1))
    return x * cos + partner * sin


def _proj_rope_kernel(x_ref, w_ref, *rest, modes, splits, n_tab):
    tabs = rest[:2 * n_tab]
    outs = rest[2 * n_tab:]
    acc = _dot(x_ref[...], w_ref[...])
    g = 0
    for out_ref, n_groups in zip(outs, splits):
        for local in range(n_groups):
            blk = acc[:, g * LANE:(g + 1) * LANE]
            if modes[g] is not None:
                tid, width = modes[g]
                blk = _rope_group(blk, tabs[2 * tid][...], tabs[2 * tid + 1][...], width)
            out_ref[:, local * LANE:(local + 1) * LANE] = blk.astype(BF16)
            g += 1
    outs[-1][...] = acc[:, -LANE:]


def _proj_rope(x, w, tabs, modes, splits, seq, tm, name):
    m, k = x.shape
    n = w.shape[1]
    per_seq = seq // tm
    in_specs = [pl.BlockSpec((tm, k), lambda i: (i, 0)), pl.BlockSpec((k, n), lambda i: (0, 0))]
    args = [x, w]
    for cos, sin in tabs:
        in_specs += [pl.BlockSpec((tm, LANE), lambda i: (i % per_seq, 0))] * 2
        args += [cos, sin]
    out_specs = [pl.BlockSpec((tm, ng * LANE), lambda i: (i, 0)) for ng in splits]
    out_shape = [jax.ShapeDtypeStruct((m, ng * LANE), BF16) for ng in splits]
    out_specs.append(pl.BlockSpec((tm, LANE), lambda i: (i, 0)))
    out_shape.append(jax.ShapeDtypeStruct((m, LANE), F32))
    return pl.pallas_call(
        functools.partial(_proj_rope_kernel, modes=modes, splits=splits, n_tab=len(tabs)),
        grid=(m // tm,),
        in_specs=in_specs,
        out_specs=out_specs,
        out_shape=out_shape,
        compiler_params=_cparams(("parallel",), 48),
        name=name,
    )(*args)


def _ffn_kernel(x_ref, wa_ref, wg_ref, wo_ref, g_ref, b_ref, *rest, nf, tf, scale, want_bf16):
    if want_bf16:
        of_ref, ob_ref = rest
        xb_ref = ob_ref
    else:
        of_ref, xb_ref = rest
    j = pl.program_id(1)

    @pl.when(j == 0)
    def _():
        xb_ref[...] = x_ref[...].astype(BF16)

    subs = [slice(r, r + LN_SUB_ROWS) for r in range(0, of_ref.shape[0], LN_SUB_ROWS)]

    def hidden_slice(phase):
        w_in = jnp.concatenate([wa_ref[...], wg_ref[...]], axis=1).astype(BF16)
        wo = wo_ref[...].astype(BF16)
        ags = [_dot(xb_ref[rows, :], w_in) for rows in subs]
        hs = [(ag[:, :tf] * jax.nn.sigmoid(ag[:, :tf]) * ag[:, tf:]).astype(BF16) for ag in ags]
        parts = [_dot(h, wo) for h in hs]
        for rows, part in zip(subs, parts):
            if phase == "first":
                of_ref[rows, :] = part
            elif phase == "middle":
                of_ref[rows, :] += part
            else:
                y = _layer_norm(ALPHA * x_ref[rows, :] + scale * (of_ref[rows, :] + part), g_ref[...], b_ref[...])
                of_ref[rows, :] = y
                if want_bf16:
                    ob_ref[rows, :] = y.astype(BF16)

    pl.when(j == 0)(functools.partial(hidden_slice, "first"))
    pl.when(jnp.logical_and(j > 0, j < nf - 1))(functools.partial(hidden_slice, "middle"))
    pl.when(j == nf - 1)(functools.partial(hidden_slice, "last"))


def _ffn(xf, w_in, w_out, lead, g, b, tm, tf, scale, want_bf16):
    m, d = xf.shape
    nf = FFN_DIM // tf
    nl = (None,) * len(lead)
    row = lambda i, j: (i, 0)
    const = lambda i, j: (0, 0)
    out_specs = [pl.BlockSpec((tm, d), row)]
    out_shape = [jax.ShapeDtypeStruct((m, d), F32)]
    if want_bf16:
        out_specs.append(pl.BlockSpec((tm, d), row))
        out_shape.append(jax.ShapeDtypeStruct((m, d), BF16))
    return pl.pallas_call(
        functools.partial(_ffn_kernel, nf=nf, tf=tf, scale=scale, want_bf16=want_bf16),
        grid=(m // tm, nf),
        in_specs=[pl.BlockSpec((tm, d), row),
                  pl.BlockSpec(nl + (d, tf), lambda i, j: lead + (0, j)),
                  pl.BlockSpec(nl + (d, tf), lambda i, j: lead + (0, j + nf)),
                  pl.BlockSpec(nl + (tf, d), lambda i, j: lead + (j, 0)),
                  pl.BlockSpec((1, d), const), pl.BlockSpec((1, d), const)],
        out_specs=out_specs,
        out_shape=out_shape,
        scratch_shapes=[] if want_bf16 else [pltpu.VMEM((tm, d), BF16)],
        compiler_params=_cparams(("parallel", "arbitrary"), 60),
        name="ffn_fused",
    )(xf, w_in, w_in, w_out, g, b)


def _out_ln_kernel(*refs, n_lhs):
    lhs = refs[:n_lhs]
    ws = refs[n_lhs:2 * n_lhs]
    x_ref, g_ref, b_ref, of_ref, ob_ref = refs[2 * n_lhs:]
    for r in range(0, of_ref.shape[0], LN_SUB_ROWS):
        rows = slice(r, r + LN_SUB_ROWS)
        acc = _dot(lhs[0][rows, :], ws[0][...])
        for l_ref, w_ref in zip(lhs[1:], ws[1:]):
            acc = acc + _dot(l_ref[rows, :], w_ref[...])
        y = _layer_norm(ALPHA * x_ref[rows, :] + acc, g_ref[...], b_ref[...])
        of_ref[rows, :] = y
        ob_ref[rows, :] = y.astype(BF16)


def _out_ln(lhs_list, w, xf, g, b, tm, name):
    m, n = xf.shape
    row = lambda i: (i, 0)
    in_specs, args, off = [], [], 0
    for l in lhs_list:
        in_specs.append(pl.BlockSpec((tm, l.shape[1]), row))
        args.append(l)
    for l in lhs_list:
        kl = l.shape[1]
        in_specs.append(pl.BlockSpec((kl, n), functools.partial(lambda i, o: (o, 0), o=off // kl)))
        args.append(w)
        off += kl
    in_specs += [pl.BlockSpec((tm, n), row), pl.BlockSpec((1, n), lambda i: (0, 0)), pl.BlockSpec((1, n), lambda i: (0, 0))]
    args += [xf, g, b]
    return pl.pallas_call(
        functools.partial(_out_ln_kernel, n_lhs=len(lhs_list)),
        grid=(m // tm,),
        in_specs=in_specs,
        out_specs=[pl.BlockSpec((tm, n), row), pl.BlockSpec((tm, n), row)],
        out_shape=[jax.ShapeDtypeStruct((m, n), F32), jax.ShapeDtypeStruct((m, n), BF16)],
        compiler_params=_cparams(("parallel",), 56),
        name=name,
    )(*args)


def _dsa_kernel(q_ref, k_ref, v_ref, qi_ref, ki_ref, w_ref, o_ref, plane_ref, bias_ref, *, tq, seq, topk):
    i = pl.program_id(1)
    per_span = DSA_KEY_SPAN // tq
    for n in range(1, seq // DSA_KEY_SPAN + 1):
        pl.when(i // per_span == n - 1)(functools.partial(
            _dsa_body, i, q_ref, k_ref, v_ref, qi_ref, ki_ref, w_ref, o_ref, plane_ref, bias_ref,
            tq=tq, nk=n * DSA_KEY_SPAN, topk=topk))


def _dsa_body(i, q_ref, k_ref, v_ref, qi_ref, ki_ref, w_ref, o_ref, plane_ref, bias_ref, *, tq, nk, topk):
    span = DSA_KEY_SPAN
    n_groups = nk // BIT_GROUP
    w_t = w_ref[...].T
    key_pos = nk - span + lax.broadcasted_iota(jnp.int32, (span, tq), 0)
    q_pos = i * tq + lax.broadcasted_iota(jnp.int32, (span, tq), 1)
    allowed = (key_pos >> CHUNK_BITS) <= (q_pos >> CHUNK_BITS)
    qi_all = jnp.concatenate([qi_ref[:, h * IDX_DIM:(h + 1) * IDX_DIM] for h in range(IDX_HEADS)], axis=0)
    for c in range(nk // span):
        rel = jnp.maximum(_dot_nt(ki_ref[c * span:(c + 1) * span, :IDX_DIM], qi_all), 0.0)
        score = jnp.zeros((span, tq), F32)
        for h in range(IDX_HEADS):
            score = score + rel[:, h * tq:(h + 1) * tq] * w_t[W_IDX_LANE + h:W_IDX_LANE + h + 1, :]
        if c == nk // span - 1:
            score = jnp.where(allowed, score, NEG_INF)
        bits = lax.bitcast_convert_type(score, jnp.int32)
        u = bits ^ ((bits >> 31) | INT_MIN)
        for gg in range(span // BIT_GROUP):
            words = [u[gg * BIT_GROUP + 8 * j:gg * BIT_GROUP + 8 * j + 8, :] for j in range(32)]
            g = c * (span // BIT_GROUP) + gg
            for b, plane in enumerate(_bit_transpose32(words)):
                plane_ref[g, b] = plane

    def select_bit(j, state):
        alive, keep, above = state[:n_groups], state[n_groups:2 * n_groups], state[-1]
        ones = [a & plane_ref[g, 31 - j] for g, a in enumerate(alive)]
        cnt = lax.population_count(ones[0])
        for o in ones[1:]:
            cnt = cnt + lax.population_count(o)
        for shift in (4, 2, 1):
            cnt = cnt + pltpu.roll(cnt, shift, 0)
        take = (above + cnt) >= topk
        zeros = [a ^ o for a, o in zip(alive, ones)]
        new_alive = [jnp.where(take, o, z) for o, z in zip(ones, zeros)]
        new_keep = [jnp.where(take, kp ^ z, kp) for kp, z in zip(keep, zeros)]
        return tuple(new_alive) + tuple(new_keep) + (jnp.where(take, above, above + cnt),)

    everything = jnp.full((8, tq), -1, jnp.int32)
    state = lax.fori_loop(0, 32, select_bit, (everything,) * (2 * n_groups) + (jnp.zeros((8, tq), jnp.int32),))
    alive, keep, above = state[:n_groups], state[n_groups:2 * n_groups], state[-1]

    need = topk - above
    sub = lax.broadcasted_iota(jnp.int32, (8, tq), 0)

    def below(limit, g):
        n_bits = jnp.clip((limit - g * BIT_GROUP - sub + 7) >> 3, 0, 32)
        return jnp.where(n_bits >= 32, -1, lax.shift_left(jnp.int32(1), jnp.minimum(n_bits, 31)) - 1)

    bound = jnp.zeros((8, tq), jnp.int32)
    step = 1 << ((nk - 1).bit_length() - 1)
    while step >= 1:
        cnt = lax.population_count(alive[0] & below(bound + step, 0))
        for g in range(1, n_groups):
            cnt = cnt + lax.population_count(alive[g] & below(bound + step, g))
        for shift in (4, 2, 1):
            cnt = cnt + pltpu.roll(cnt, shift, 0)
        bound = jnp.where(cnt < need, bound + step, bound)
        step //= 2
    keep = [kp ^ (a & ~below(bound + 1, g)) for g, (kp, a) in enumerate(zip(keep, alive))]

    for g in range(n_groups):
        for j in range(32):
            rows = slice(g * BIT_GROUP + 8 * j, g * BIT_GROUP + 8 * j + 8)
            bias = jnp.where(lax.shift_left(keep[g], 31 - j) < 0, 0.0, NEG_INF)
            if g * BIT_GROUP + 8 * j >= nk - span:
                bias = jnp.where(allowed[rows.start - (nk - span):rows.stop - (nk - span), :], bias, NEG_INF)
            bias_ref[rows, :] = bias

    k = k_ref[:nk, :]
    v_ext = jnp.concatenate([v_ref[:nk, :], jnp.ones((nk, HEAD_DIM), BF16)], axis=1)
    groups = [[slice(h * HEAD_DIM, (h + 1) * HEAD_DIM) for h in range(g * DSA_HEAD_GROUP, (g + 1) * DSA_HEAD_GROUP)]
              for g in range(N_HEADS_A // DSA_HEAD_GROUP)]
    logits = [_dot_nt(k, jnp.concatenate([q_ref[:, cols] for cols in heads], axis=0)) * SCALE for heads in groups]
    probs = []
    for lg in logits:
        lg = jnp.concatenate([lg[:, j * tq:(j + 1) * tq] + bias_ref[:nk, :] for j in range(DSA_HEAD_GROUP)], axis=1)
        probs.append(jnp.exp(lg - jnp.max(lg, axis=0, keepdims=True)).astype(BF16))
    outs = [lax.dot_general(e, v_ext, (((0,), (0,)), ((), ())), preferred_element_type=F32) for e in probs]
    for heads, o in zip(groups, outs):
        o = o[:, :HEAD_DIM] / o[:, HEAD_DIM:HEAD_DIM + 1]
        for j, cols in enumerate(heads):
            o_ref[:, cols] = o[j * tq:(j + 1) * tq].astype(BF16)


def _bit_transpose32(words):
    words = list(words)
    for dist, low in ((16, 0x0000FFFF), (8, 0x00FF00FF), (4, 0x0F0F0F0F), (2, 0x33333333), (1, 0x55555555)):
        for k in range(32):
            if k & dist:
                continue
            a, b = words[k], words[k + dist]
            t = (lax.shift_right_logical(a, dist) ^ b) & low
            words[k + dist] = b ^ t
            words[k] = a ^ lax.shift_left(t, dist)
    return words


def _dsa_attention(qkv_a, qk_i, w_i, batch, seq):
    tq = 2 * CHUNK
    nq = seq // tq
    hq = N_HEADS_A * HEAD_DIM
    kcol = hq // LANE
    icol = (IDX_HEADS * IDX_DIM) // LANE
    qmap = lambda b, i: (b * nq + i, 0)
    return pl.pallas_call(
        functools.partial(_dsa_kernel, tq=tq, seq=seq, topk=min(TOPK_MAX, seq // 4)),
        grid=(batch, nq),
        in_specs=[pl.BlockSpec((tq, hq), qmap),
                  pl.BlockSpec((seq, LANE), lambda b, i: (b, kcol)),
                  pl.BlockSpec((seq, LANE), lambda b, i: (b, kcol + 1)),
                  pl.BlockSpec((tq, IDX_HEADS * IDX_DIM), qmap),
                  pl.BlockSpec((seq, LANE), lambda b, i: (b, icol)),
                  pl.BlockSpec((tq, LANE), qmap)],
        out_specs=pl.BlockSpec((tq, hq), qmap),
        out_shape=jax.ShapeDtypeStruct((batch * seq, hq), BF16),
        scratch_shapes=[pltpu.VMEM((seq // BIT_GROUP, 32, 8, tq), jnp.int32), pltpu.VMEM((seq, tq), F32)],
        compiler_params=_cparams(("parallel", "parallel"), 48),
        name="dsa_attention",
    )(qkv_a, qkv_a, qkv_a, qk_i, qk_i, w_i)


def _band_bias_kernel(rel_ref, o_ref):
    h = pl.program_id(0)
    i = lax.broadcasted_iota(jnp.int32, (BAND_Q, BAND_Q), 0)
    jj = lax.broadcasted_iota(jnp.int32, (BAND_Q, BAND_Q), 1)
    for jb in range(BAND_KB):
        j = jb * BAND_Q + jj
        idx = jnp.clip(i - j + LEFT_CHUNKS * CHUNK, -(CHUNK - 1), REL_CLIP) + (CHUNK - 1)
        cq = i >> CHUNK_BITS
        ck = j >> CHUNK_BITS
        d_lo = LEFT_CHUNKS * CHUNK - (jb + 1) * BAND_Q + 1
        d_hi = LEFT_CHUNKS * CHUNK - jb * BAND_Q + BAND_Q - 1
        r_lo = min(max(d_lo, -(CHUNK - 1)), REL_CLIP) + (CHUNK - 1)
        r_hi = min(max(d_hi, -(CHUNK - 1)), REL_CLIP) + (CHUNK - 1)
        val = lax.fori_loop(r_lo, r_hi + 1, lambda r, acc: jnp.where(idx == r, rel_ref[h, r], acc),
                            jnp.zeros((BAND_Q, BAND_Q), F32))
        val = jnp.where(ck >= cq, val, NEG_INF)
        o_ref[0, jb] = jnp.where(ck <= cq + LEFT_CHUNKS, val, NEG_INF)


def _band_bias(rel_bias):
    nh = rel_bias.shape[0]
    return pl.pallas_call(
        _band_bias_kernel,
        grid=(nh,),
        in_specs=[pl.BlockSpec(memory_space=pltpu.SMEM)],
        out_specs=pl.BlockSpec((1, BAND_KB, BAND_Q, BAND_Q), lambda h: (h, 0, 0, 0)),
        out_shape=jax.ShapeDtypeStruct((nh, BAND_KB, BAND_Q, BAND_Q), F32),
        name="band_bias",
    )(rel_bias)


def _band_kernel(q_ref, k_ref, v_ref, bias_ref, o_ref):
    p = pl.program_id(1)
    first = BAND_KB - 1
    shift = jnp.maximum(first - p, 0)
    s0 = pl.multiple_of(jnp.maximum(p - first, 0) * BAND_Q, BAND_Q)
    heads = [slice(h * HEAD_DIM, (h + 1) * HEAD_DIM) for h in range(N_HEADS_B)]
    logits = [_dot_nt(q_ref[:, cols], k_ref[pl.ds(s0, BAND_W), cols]) * SCALE for cols in heads]
    probs, sums = [], []
    for h, lg in enumerate(logits):
        blocks = []
        for jb in range(BAND_KB):
            src = jb + shift
            blk = lg[:, jb * BAND_Q:(jb + 1) * BAND_Q] + bias_ref[h, jnp.minimum(src, first)]
            blocks.append(jnp.where(src <= first, blk, NEG_INF))
        lg = jnp.concatenate(blocks, axis=1)
        e = jnp.exp(lg - jnp.max(lg, axis=1, keepdims=True))
        sums.append(jnp.sum(e, axis=1, keepdims=True))
        probs.append(e.astype(BF16))
    outs = [_dot(e, v_ref[pl.ds(s0, BAND_W), cols]) for e, cols in zip(probs, heads)]
    for o, s, cols in zip(outs, sums, heads):
        o_ref[:, cols] = (o / s).astype(BF16)


def _band_attention(qkv_b, bias, batch, seq):
    nq = seq // BAND_Q
    hq = N_HEADS_B * HEAD_DIM
    qmap = lambda b, p: (b * nq + p, 0)
    return pl.pallas_call(
        _band_kernel,
        grid=(batch, nq),
        in_specs=[pl.BlockSpec((BAND_Q, hq), qmap),
                  pl.BlockSpec((seq, hq), lambda b, p: (b, 1)),
                  pl.BlockSpec((seq, hq), lambda b, p: (b, 2)),
                  pl.BlockSpec(bias.shape, lambda b, p: (0, 0, 0, 0))],
        out_specs=pl.BlockSpec((BAND_Q, hq), qmap),
        out_shape=jax.ShapeDtypeStruct((batch * seq, hq), BF16),
        compiler_params=_cparams(("parallel", "parallel"), 48),
        name="band_attention",
    )(qkv_b, qkv_b, qkv_b, bias)


def _sb_blocks(qs, k_blks, v_blks, later, carry, past):
    zs = [_dot_nt(q, k) * (SCALE * LOG2_E) for q, k in zip(qs, k_blks)]
    log_betas, drops, splits = [], [], []
    for z2 in zs:
        neg_abs = lax.bitcast_convert_type(lax.bitcast_convert_type(z2, jnp.int32) | INT_MIN, F32)
        drop = jnp.maximum(z2, 0.0) + jnp.log2(1.0 + jnp.exp2(neg_abs))
        log_betas.append(z2 - drop)
        if past is not None:
            drop = jnp.where(past, drop, 0.0)
        drops.append(drop)
        hi = drop.astype(BF16)
        r1 = drop - hi.astype(F32)
        mid = r1.astype(BF16)
        lo = (r1 - mid.astype(F32)).astype(BF16)
        splits.append(jnp.concatenate([hi, mid, lo], axis=1))
    afters = [_dot(s, later) for s in splits]
    probs, tails = [], []
    for h, (log_beta, drop, after) in enumerate(zip(log_betas, drops, afters)):
        row_sum = jnp.sum(drop, axis=1, keepdims=True)
        if carry is None:
            a = jnp.exp2(log_beta - after)
            tails.append(row_sum)
        else:
            a = jnp.exp2(log_beta - carry[2 * h] - after)
            tails.append(carry[2 * h] + row_sum)
        if past is not None:
            a = jnp.where(past, a, 0.0)
        probs.append(a.astype(BF16))
    pvs = [_dot(a, v) for a, v in zip(probs, v_blks)]
    out = []
    for h, (tail, pv) in enumerate(zip(tails, pvs)):
        out += [tail, pv if carry is None else carry[2 * h + 1] + pv]
    return tuple(out)


def _sb_kernel(q_ref, k_ref, v_ref, o_ref, *, tq, nh):
    i = pl.program_id(2)
    r3 = lax.broadcasted_iota(jnp.int32, (3 * tq, tq), 0)
    c3 = lax.broadcasted_iota(jnp.int32, (3 * tq, tq), 1)
    later = jnp.where((r3 & (tq - 1)) > c3, 1.0, 0.0).astype(BF16)
    row = lax.broadcasted_iota(jnp.int32, (tq, tq), 0)
    col = lax.broadcasted_iota(jnp.int32, (tq, tq), 1)
    heads = [slice(h * HEAD_DIM, (h + 1) * HEAD_DIM) for h in range(nh)]

    def walk(s0, carry, past):
        return _sb_blocks([q_ref[:, cols] for cols in heads],
                          [k_ref[pl.ds(s0, tq), cols] for cols in heads],
                          [v_ref[pl.ds(s0, tq), cols] for cols in heads], later, carry, past)

    def least_tail(c):
        m = c[0]
        for h in range(1, nh):
            m = jnp.minimum(m, c[2 * h])
        return jnp.min(m)

    def alive(state):
        return jnp.logical_and(state[0] < i, state[1] < SB_DEAD_BITS)

    def step(state):
        c = walk(pl.multiple_of((i - 1 - state[0]) * tq, tq), state[2:], None)
        return (state[0] + 1, least_tail(c)) + c

    carry = walk(pl.multiple_of(i * tq, tq), None, col < row)
    carry = lax.while_loop(alive, step, (jnp.int32(0), least_tail(carry)) + carry)[2:]
    for h, cols in enumerate(heads):
        o_ref[:, cols] = carry[2 * h + 1].astype(BF16)


def _stick_breaking(qkv, batch, seq, tq, nh):
    nq = seq // tq
    ng = N_HEADS_C // nh
    w = nh * HEAD_DIM
    qmap = lambda b, g, i: (b * nq + i, g)
    return pl.pallas_call(
        functools.partial(_sb_kernel, tq=tq, nh=nh),
        grid=(batch, ng, nq),
        in_specs=[pl.BlockSpec((tq, w), qmap),
                  pl.BlockSpec((seq, w), lambda b, g, i: (b, ng + g)),
                  pl.BlockSpec((seq, w), lambda b, g, i: (b, 2 * ng + g))],
        out_specs=pl.BlockSpec((tq, w), qmap),
        out_shape=jax.ShapeDtypeStruct((batch * seq, N_HEADS_C * HEAD_DIM), BF16),
        compiler_params=_cparams(("parallel", "parallel", "parallel"), 32),
        name="stick_breaking",
    )(qkv, qkv, qkv)


def _xattn_kernel(xb_ref, xf_ref, kv_ref, wq_ref, wo_ref, g_ref, b_ref, of_ref, wqb_ref, wob_ref):
    @pl.when(pl.program_id(0) == 0)
    def _():
        wqb_ref[...] = wq_ref[...].astype(BF16)
        wob_ref[...] = wo_ref[...].astype(BF16)

    hk = N_HEADS_X * HEAD_DIM
    heads = [slice(h * HEAD_DIM, (h + 1) * HEAD_DIM) for h in range(N_HEADS_X)]
    for r in range(0, of_ref.shape[0], LN_SUB_ROWS):
        rows = slice(r, r + LN_SUB_ROWS)
        q = _dot(xb_ref[rows, :], wqb_ref[...]).astype(BF16)
        logits = [_dot_nt(q[:, cols], kv_ref[:, cols]) * SCALE for cols in heads]
        probs, sums = [], []
        for lg in logits:
            e = jnp.exp(lg - jnp.max(lg, axis=1, keepdims=True))
            sums.append(jnp.sum(e, axis=1, keepdims=True))
            probs.append(e.astype(BF16))
        outs = [_dot(e, kv_ref[:, hk + h * HEAD_DIM:hk + (h + 1) * HEAD_DIM]) for h, e in enumerate(probs)]
        o = jnp.concatenate([(oh / s).astype(BF16) for oh, s in zip(outs, sums)], axis=1)
        of_ref[rows, :] = _layer_norm(ALPHA * xf_ref[rows, :] + _dot(o, wob_ref[...]), g_ref[...], b_ref[...])


def _xattn(xb, xf, kv, wq, wo, lead, g, b, seq, tm):
    m, n = xf.shape
    per_seq = seq // tm
    nl = (None,) * len(lead)
    row = lambda i: (i, 0)
    const = lambda i: (0, 0)
    return pl.pallas_call(
        _xattn_kernel,
        grid=(m // tm,),
        in_specs=[pl.BlockSpec((tm, n), row), pl.BlockSpec((tm, n), row),
                  pl.BlockSpec((MEM_LEN, kv.shape[1]), lambda i: (i // per_seq, 0)),
                  pl.BlockSpec(nl + wq.shape[-2:], lambda i: lead + (0, 0)),
                  pl.BlockSpec(nl + wo.shape[-2:], lambda i: lead + (0, 0)),
                  pl.BlockSpec((1, n), const), pl.BlockSpec((1, n), const)],
        out_specs=pl.BlockSpec((tm, n), row),
        out_shape=jax.ShapeDtypeStruct((m, n), F32),
        scratch_shapes=[pltpu.VMEM(wq.shape[-2:], BF16), pltpu.VMEM(wo.shape[-2:], BF16)],
        compiler_params=_cparams(("arbitrary",), 56),
        name="xattn_ln",
    )(xb, xf, kv, wq, wo, g, b)


def _rope_tables(seq):
    pos = np.arange(seq, dtype=np.float64)[:, None]

    def half_tables(half):
        inv_freq = ROPE_THETA ** (-np.arange(half, dtype=np.float64) / half)
        ang = pos * inv_freq[None, :]
        return np.cos(ang).astype(np.float32), np.sin(ang).astype(np.float32)

    c64, s64 = half_tables(HEAD_DIM // 2)
    c32, s32 = half_tables(IDX_DIM // 2)
    rest = LANE - IDX_DIM
    ones, zeros = np.ones((seq, rest), np.float32), np.zeros((seq, rest), np.float32)
    head = (np.concatenate([c64, c64], 1), np.concatenate([-s64, s64], 1))
    idx = (np.concatenate([c32, c32, c32, c32], 1), np.concatenate([-s32, s32, -s32, s32], 1))
    idx_key = (np.concatenate([c32, c32, ones], 1), np.concatenate([-s32, s32, zeros], 1))
    return tuple(tuple(jnp.asarray(t) for t in pair) for pair in (head, idx, idx_key))


def kernel(x, mem, ln_g, ln_b, ffn_in, ffn_out, xattn_q, xattn_kv, xattn_o,
           even_in, even_out, even_rel_bias, odd_in, odd_out):
    batch, seq, d = x.shape
    t = batch * seq
    xf = x.reshape(t, d)
    memb = mem.reshape(batch * MEM_LEN, d).astype(BF16)
    tab_head, tab_idx, tab_idx_key = _rope_tables(seq)

    modes_a = ((0, HEAD_DIM),) * (N_HEADS_A + 1) + (None,)
    modes_i = ((1, IDX_DIM),) * ((IDX_HEADS * IDX_DIM) // LANE) + ((2, IDX_DIM),)

    def ln_params(layer, j):
        return ln_g[layer, j][None, :], ln_b[layer, j][None, :]

    def ffn(xf, layer, j):
        g, b = ln_params(layer, 3 * j)
        outs = _ffn(xf, ffn_in, ffn_out, (layer, j), g, b, 1024, 256, 0.5, want_bf16=(j == 0))
        return outs[0], (outs[1] if j == 0 else None)

    for layer in range(DEPTH):
        xf, xb = ffn(xf, layer, 0)

        g, b = ln_params(layer, 1)
        if layer % 2 == 0:
            w_in = even_in[layer // 2]
            w_ai = jnp.pad(w_in[:, :A_COLS + I_COLS], ((0, 0), (0, I_PAD - I_COLS))).astype(BF16)
            w_b = w_in[:, A_COLS + I_COLS:].astype(BF16)
            qkv_a, qk_i, w_idx = _proj_rope(xb, w_ai, [tab_head, tab_idx, tab_idx_key], modes_a + modes_i,
                                            (len(modes_a), len(modes_i)), seq, 512, "proj_dsa")
            qkv_b = _matmul(xb, w_b, 2048, 1024, "proj_band")
            o_a = _dsa_attention(qkv_a, qk_i, w_idx, batch, seq)
            o_b = _band_attention(qkv_b, _band_bias(even_rel_bias[layer // 2]), batch, seq)
            xf, xb = _out_ln([o_a, o_b], even_out[layer // 2].astype(BF16), xf, g, b, 512, "even_out_ln")
        else:
            qkv = _matmul(xb, odd_in, 1024, 1024, "proj_odd", lead=(layer // 2,))
            o = _stick_breaking(qkv, batch, seq, 256, 4)
            xf, xb = _out_ln([o], odd_out[layer // 2].astype(BF16), xf, g, b, 512, "odd_out_ln")

        g, b = ln_params(layer, 2)
        kv = _matmul(memb, xattn_kv, 1024, 1024, "proj_mem", lead=(layer,))
        xf = _xattn(xb, xf, kv, xattn_q, xattn_o, (layer,), g, b, seq, 512)

        xf, _ = ffn(xf, layer, 1)

    return xf.reshape(batch, seq, d)
```

```python
import functools

import jax
import jax.numpy as jnp
import numpy as np
from jax import lax
from jax.experimental import pallas as pl
from jax.experimental.pallas import tpu as pltpu

F32 = jnp.float32
BF16 = jnp.bfloat16

D_MODEL = 2048
DEPTH = 4
CHUNK = 64
CHUNK_BITS = CHUNK.bit_length() - 1
MEM_LEN = 256
HEAD_DIM = 128
ROPE_THETA = 10000.0
LN_EPS = 1e-5
NEG_INF = -1e30
N_HEADS_A = 8
N_HEADS_B = 8
IDX_HEADS = 16
IDX_DIM = 64
TOPK_MAX = 256
LEFT_CHUNKS = 8
REL_CLIP = 128
REL_SIZE = CHUNK + REL_CLIP
N_HEADS_C = 16
N_HEADS_X = 4
FFN_DIM = ((8 * D_MODEL // 3 + 255) // 256) * 256
ALPHA = (2.0 * DEPTH) ** 0.25
SCALE = HEAD_DIM ** -0.5
LOG2_E = 1.4426950408889634

LANE = 128
INT_MIN = -2147483648
MIB = 1024 * 1024

A_COLS = (N_HEADS_A + 2) * HEAD_DIM
I_COLS = IDX_HEADS * IDX_DIM + IDX_DIM + IDX_HEADS
I_PAD = ((I_COLS + LANE - 1) // LANE) * LANE
W_IDX_LANE = IDX_DIM

BAND_Q = 2 * CHUNK
BAND_KB = (LEFT_CHUNKS * CHUNK) // BAND_Q + 1
BAND_W = BAND_KB * BAND_Q

LN_SUB_ROWS = 256

SB_DEAD_BITS = 152.0

DSA_KEY_SPAN = 256
DSA_HEAD_GROUP = 4
BIT_GROUP = 32 * 8


def _cparams(sem, vmem_mib):
    return pltpu.CompilerParams(dimension_semantics=sem, vmem_limit_bytes=vmem_mib * MIB)


def _dot(a, b):
    return jnp.dot(a, b, preferred_element_type=F32)


def _dot_nt(a, b):
    return lax.dot_general(a, b, (((1,), (1,)), ((), ())), preferred_element_type=F32)


def _layer_norm(y, g, b):
    mu = jnp.mean(y, axis=-1, keepdims=True)
    d = y - mu
    var = jnp.mean(d * d, axis=-1, keepdims=True)
    return d * lax.rsqrt(var + LN_EPS) * g + b


def _mm_kernel(x_ref, w_ref, o_ref, *scratch):
    if scratch:
        wb_ref, = scratch

        @pl.when(pl.program_id(1) == 0)
        def _():
            wb_ref[...] = w_ref[...].astype(BF16)
    else:
        wb_ref = w_ref
    o_ref[...] = _dot(x_ref[...], wb_ref[...]).astype(o_ref.dtype)


def _matmul(x, w, tm, tn, name, lead=()):
    m, k = x.shape
    n = w.shape[-1]
    cast = w.dtype != BF16
    return pl.pallas_call(
        _mm_kernel,
        grid=(n // tn, m // tm),
        in_specs=[pl.BlockSpec((tm, k), lambda j, i: (i, 0)),
                  pl.BlockSpec((None,) * len(lead) + (k, tn), lambda j, i: lead + (0, j))],
        out_specs=pl.BlockSpec((tm, tn), lambda j, i: (i, j)),
        out_shape=jax.ShapeDtypeStruct((m, n), BF16),
        scratch_shapes=[pltpu.VMEM((k, tn), BF16)] if cast else [],
        compiler_params=_cparams(("parallel", "arbitrary"), 48),
        name=name,
    )(x, w)


def _rope_group(x, cos, sin, width):
    if width == LANE:
        partner = pltpu.roll(x, LANE // 2, 1)
    else:
        lane = lax.broadcasted_iota(jnp.int32, x.shape, 1)
        half = width // 2
        partner = jnp.where((lane & half) == 0, pltpu.roll(x, LANE - half, 1), pltpu.roll(x, half, 1))
    return x * cos + partner * sin


def _proj_rope_kernel(x_ref, w_ref, *rest, modes, splits, n_tab):
    tabs = rest[:2 * n_tab]
    outs = rest[2 * n_tab:]
    acc = _dot(x_ref[...], w_ref[...])
    g = 0
    for out_ref, n_groups in zip(outs, splits):
        for local in range(n_groups):
            blk = acc[:, g * LANE:(g + 1) * LANE]
            if modes[g] is not None:
                tid, width = modes[g]
                blk = _rope_group(blk, tabs[2 * tid][...], tabs[2 * tid + 1][...], width)
            out_ref[:, local * LANE:(local + 1) * LANE] = blk.astype(BF16)
            g += 1
    outs[-1][...] = acc[:, -LANE:]


def _proj_rope(x, w, tabs, modes, splits, seq, tm, name):
    m, k = x.shape
    n = w.shape[1]
    per_seq = seq // tm
    in_specs = [pl.BlockSpec((tm, k), lambda i: (i, 0)), pl.BlockSpec((k, n), lambda i: (0, 0))]
    args = [x, w]
    for cos, sin in tabs:
        in_specs += [pl.BlockSpec((tm, LANE), lambda i: (i % per_seq, 0))] * 2
        args += [cos, sin]
    out_specs = [pl.BlockSpec((tm, ng * LANE), lambda i: (i, 0)) for ng in splits]
    out_shape = [jax.ShapeDtypeStruct((m, ng * LANE), BF16) for ng in splits]
    out_specs.append(pl.BlockSpec((tm, LANE), lambda i: (i, 0)))
    out_shape.append(jax.ShapeDtypeStruct((m, LANE), F32))
    return pl.pallas_call(
        functools.partial(_proj_rope_kernel, modes=modes, splits=splits, n_tab=len(tabs)),
        grid=(m // tm,),
        in_specs=in_specs,
        out_specs=out_specs,
        out_shape=out_shape,
        compiler_params=_cparams(("parallel",), 48),
        name=name,
    )(*args)


def _ffn_kernel(x_ref, wa_ref, wg_ref, wo_ref, g_ref, b_ref, *rest, nf, tf, scale, want_bf16):
    if want_bf16:
        of_ref, ob_ref = rest
        xb_ref = ob_ref
    else:
        of_ref, xb_ref = rest
    j = pl.program_id(1)

    @pl.when(j == 0)
    def _():
        xb_ref[...] = x_ref[...].astype(BF16)

    subs = [slice(r, r + LN_SUB_ROWS) for r in range(0, of_ref.shape[0], LN_SUB_ROWS)]

    def hidden_slice(phase):
        w_in = jnp.concatenate([wa_ref[...], wg_ref[...]], axis=1).astype(BF16)
        wo = wo_ref[...].astype(BF16)
        ags = [_dot(xb_ref[rows, :], w_in) for rows in subs]
        hs = [(ag[:, :tf] * jax.nn.sigmoid(ag[:, :tf]) * ag[:, tf:]).astype(BF16) for ag in ags]
        parts = [_dot(h, wo) for h in hs]
        for rows, part in zip(subs, parts):
            if phase == "first":
                of_ref[rows, :] = part
            elif phase == "middle":
                of_ref[rows, :] += part
            else:
                y = _layer_norm(ALPHA * x_ref[rows, :] + scale * (of_ref[rows, :] + part), g_ref[...], b_ref[...])
                of_ref[rows, :] = y
                if want_bf16:
                    ob_ref[rows, :] = y.astype(BF16)

    pl.when(j == 0)(functools.partial(hidden_slice, "first"))
    pl.when(jnp.logical_and(j > 0, j < nf - 1))(functools.partial(hidden_slice, "middle"))
    pl.when(j == nf - 1)(functools.partial(hidden_slice, "last"))


def _ffn(xf, w_in, w_out, lead, g, b, tm, tf, scale, want_bf16):
    m, d = xf.shape
    nf = FFN_DIM // tf
    nl = (None,) * len(lead)
    row = lambda i, j: (i, 0)
    const = lambda i, j: (0, 0)
    out_specs = [pl.BlockSpec((tm, d), row)]
    out_shape = [jax.ShapeDtypeStruct((m, d), F32)]
    if want_bf16:
        out_specs.append(pl.BlockSpec((tm, d), row))
        out_shape.append(jax.ShapeDtypeStruct((m, d), BF16))
    return pl.pallas_call(
        functools.partial(_ffn_kernel, nf=nf, tf=tf, scale=scale, want_bf16=want_bf16),
        grid=(m // tm, nf),
        in_specs=[pl.BlockSpec((tm, d), row),
                  pl.BlockSpec(nl + (d, tf), lambda i, j: lead + (0, j)),
                  pl.BlockSpec(nl + (d, tf), lambda i, j: lead + (0, j + nf)),
                  pl.BlockSpec(nl + (tf, d), lambda i, j: lead + (j, 0)),
                  pl.BlockSpec((1, d), const), pl.BlockSpec((1, d), const)],
        out_specs=out_specs,
        out_shape=out_shape,
        scratch_shapes=[] if want_bf16 else [pltpu.VMEM((tm, d), BF16)],
        compiler_params=_cparams(("parallel", "arbitrary"), 60),
        name="ffn_fused",
    )(xf, w_in, w_in, w_out, g, b)


def _out_ln_kernel(*refs, n_lhs):
    lhs = refs[:n_lhs]
    ws = refs[n_lhs:2 * n_lhs]
    x_ref, g_ref, b_ref, of_ref, ob_ref = refs[2 * n_lhs:2 * n_lhs + 5]
    wbs = refs[-n_lhs:]

    @pl.when(pl.program_id(0) == 0)
    def _():
        for w_ref, wb_ref in zip(ws, wbs):
            wb_ref[...] = w_ref[...].astype(BF16)

    for r in range(0, of_ref.shape[0], LN_SUB_ROWS):
        rows = slice(r, r + LN_SUB_ROWS)
        acc = _dot(lhs[0][rows, :], wbs[0][...])
        for l_ref, wb_ref in zip(lhs[1:], wbs[1:]):
            acc = acc + _dot(l_ref[rows, :], wb_ref[...])
        y = _layer_norm(ALPHA * x_ref[rows, :] + acc, g_ref[...], b_ref[...])
        of_ref[rows, :] = y
        ob_ref[rows, :] = y.astype(BF16)


def _out_ln(lhs_list, w, lead, xf, g, b, tm, name):
    m, n = xf.shape
    row = lambda i: (i, 0)
    in_specs, args, off = [], [], 0
    for l in lhs_list:
        in_specs.append(pl.BlockSpec((tm, l.shape[1]), row))
        args.append(l)
    for l in lhs_list:
        kl = l.shape[1]
        in_specs.append(pl.BlockSpec((None,) * len(lead) + (kl, n),
                                     functools.partial(lambda i, o: lead + (o, 0), o=off // kl),
                                     pipeline_mode=pl.Buffered(1)))
        args.append(w)
        off += kl
    in_specs += [pl.BlockSpec((tm, n), row), pl.BlockSpec((1, n), lambda i: (0, 0)), pl.BlockSpec((1, n), lambda i: (0, 0))]
    args += [xf, g, b]
    return pl.pallas_call(
        functools.partial(_out_ln_kernel, n_lhs=len(lhs_list)),
        grid=(m // tm,),
        in_specs=in_specs,
        out_specs=[pl.BlockSpec((tm, n), row), pl.BlockSpec((tm, n), row)],
        out_shape=[jax.ShapeDtypeStruct((m, n), F32), jax.ShapeDtypeStruct((m, n), BF16)],
        scratch_shapes=[pltpu.VMEM((l.shape[1], n), BF16) for l in lhs_list],
        compiler_params=_cparams(("arbitrary",), 56),
        name=name,
    )(*args)


def _dsa_kernel(q_ref, k_ref, v_ref, qi_ref, ki_ref, w_ref, o_ref, plane_ref, bias_ref, *, tq, seq, topk):
    i = pl.program_id(1)
    per_span = DSA_KEY_SPAN // tq
    for n in range(1, seq // DSA_KEY_SPAN + 1):
        pl.when(i // per_span == n - 1)(functools.partial(
            _dsa_body, i, q_ref, k_ref, v_ref, qi_ref, ki_ref, w_ref, o_ref, plane_ref, bias_ref,
            tq=tq, nk=n * DSA_KEY_SPAN, topk=topk))


def _dsa_body(i, q_ref, k_ref, v_ref, qi_ref, ki_ref, w_ref, o_ref, plane_ref, bias_ref, *, tq, nk, topk):
    span = DSA_KEY_SPAN
    n_groups = nk // BIT_GROUP
    w_t = w_ref[...].T
    key_pos = nk - span + lax.broadcasted_iota(jnp.int32, (span, tq), 0)
    q_pos = i * tq + lax.broadcasted_iota(jnp.int32, (span, tq), 1)
    allowed = (key_pos >> CHUNK_BITS) <= (q_pos >> CHUNK_BITS)
    qi_all = jnp.concatenate([qi_ref[:, h * IDX_DIM:(h + 1) * IDX_DIM] for h in range(IDX_HEADS)], axis=0)
    for c in range(nk // span):
        rel = jnp.maximum(_dot_nt(ki_ref[c * span:(c + 1) * span, :IDX_DIM], qi_all), 0.0)
        score = jnp.zeros((span, tq), F32)
        for h in range(IDX_HEADS):
            score = score + rel[:, h * tq:(h + 1) * tq] * w_t[W_IDX_LANE + h:W_IDX_LANE + h + 1, :]
        if c == nk // span - 1:
            score = jnp.where(allowed, score, NEG_INF)
        bits = lax.bitcast_convert_type(score, jnp.int32)
        u = bits ^ ((bits >> 31) | INT_MIN)
        for gg in range(span // BIT_GROUP):
            words = [u[gg * BIT_GROUP + 8 * j:gg * BIT_GROUP + 8 * j + 8, :] for j in range(32)]
            g = c * (span // BIT_GROUP) + gg
            for b, plane in enumerate(_bit_transpose32(words)):
                plane_ref[g, b] = plane

    def select_bit(j, state):
        alive, keep, above = state[:n_groups], state[n_groups:2 * n_groups], state[-1]
        ones = [a & plane_ref[g, 31 - j] for g, a in enumerate(alive)]
        cnt = lax.population_count(ones[0])
        for o in ones[1:]:
            cnt = cnt + lax.population_count(o)
        for shift in (4, 2, 1):
            cnt = cnt + pltpu.roll(cnt, shift, 0)
        take = (above + cnt) >= topk
        zeros = [a ^ o for a, o in zip(alive, ones)]
        new_alive = [jnp.where(take, o, z) for o, z in zip(ones, zeros)]
        new_keep = [jnp.where(take, kp ^ z, kp) for kp, z in zip(keep, zeros)]
        return tuple(new_alive) + tuple(new_keep) + (jnp.where(take, above, above + cnt),)

    everything = jnp.full((8, tq), -1, jnp.int32)
    state = lax.fori_loop(0, 32, select_bit, (everything,) * (2 * n_groups) + (jnp.zeros((8, tq), jnp.int32),))
    alive, keep, above = state[:n_groups], state[n_groups:2 * n_groups], state[-1]

    need = topk - above
    sub = lax.broadcasted_iota(jnp.int32, (8, tq), 0)

    def below(limit, g):
        n_bits = jnp.clip((limit - g * BIT_GROUP - sub + 7) >> 3, 0, 32)
        return jnp.where(n_bits >= 32, -1, lax.shift_left(jnp.int32(1), jnp.minimum(n_bits, 31)) - 1)

    bound = jnp.zeros((8, tq), jnp.int32)
    step = 1 << ((nk - 1).bit_length() - 1)
    while step >= 1:
        cnt = lax.population_count(alive[0] & below(bound + step, 0))
        for g in range(1, n_groups):
            cnt = cnt + lax.population_count(alive[g] & below(bound + step, g))
        for shift in (4, 2, 1):
            cnt = cnt + pltpu.roll(cnt, shift, 0)
        bound = jnp.where(cnt < need, bound + step, bound)
        step //= 2
    keep = [kp ^ (a & ~below(bound + 1, g)) for g, (kp, a) in enumerate(zip(keep, alive))]

    for g in range(n_groups):
        for j in range(32):
            rows = slice(g * BIT_GROUP + 8 * j, g * BIT_GROUP + 8 * j + 8)
            bias = jnp.where(lax.shift_left(keep[g], 31 - j) < 0, 0.0, NEG_INF)
            if g * BIT_GROUP + 8 * j >= nk - span:
                bias = jnp.where(allowed[rows.start - (nk - span):rows.stop - (nk - span), :], bias, NEG_INF)
            bias_ref[rows, :] = bias

    k = k_ref[:nk, :]
    v_ext = jnp.concatenate([v_ref[:nk, :], jnp.ones((nk, HEAD_DIM), BF16)], axis=1)
    groups = [[slice(h * HEAD_DIM, (h + 1) * HEAD_DIM) for h in range(g * DSA_HEAD_GROUP, (g + 1) * DSA_HEAD_GROUP)]
              for g in range(N_HEADS_A // DSA_HEAD_GROUP)]
    logits = [_dot_nt(k, jnp.concatenate([q_ref[:, cols] for cols in heads], axis=0)) * SCALE for heads in groups]
    probs = []
    for lg in logits:
        lg = jnp.concatenate([lg[:, j * tq:(j + 1) * tq] + bias_ref[:nk, :] for j in range(DSA_HEAD_GROUP)], axis=1)
        probs.append(jnp.exp(lg - jnp.max(lg, axis=0, keepdims=True)).astype(BF16))
    outs = [lax.dot_general(e, v_ext, (((0,), (0,)), ((), ())), preferred_element_type=F32) for e in probs]
    for heads, o in zip(groups, outs):
        o = o[:, :HEAD_DIM] / o[:, HEAD_DIM:HEAD_DIM + 1]
        for j, cols in enumerate(heads):
            o_ref[:, cols] = o[j * tq:(j + 1) * tq].astype(BF16)


def _bit_transpose32(words):
    words = list(words)
    for dist, low in ((16, 0x0000FFFF), (8, 0x00FF00FF), (4, 0x0F0F0F0F), (2, 0x33333333), (1, 0x55555555)):
        for k in range(32):
            if k & dist:
                continue
            a, b = words[k], words[k + dist]
            t = (lax.shift_right_logical(a, dist) ^ b) & low
            words[k + dist] = b ^ t
            words[k] = a ^ lax.shift_left(t, dist)
    return words


def _dsa_attention(qkv_a, qk_i, w_i, batch, seq):
    tq = 2 * CHUNK
    nq = seq // tq
    hq = N_HEADS_A * HEAD_DIM
    kcol = hq // LANE
    icol = (IDX_HEADS * IDX_DIM) // LANE
    qmap = lambda b, i: (b * nq + i, 0)
    return pl.pallas_call(
        functools.partial(_dsa_kernel, tq=tq, seq=seq, topk=min(TOPK_MAX, seq // 4)),
        grid=(batch, nq),
        in_specs=[pl.BlockSpec((tq, hq), qmap),
                  pl.BlockSpec((seq, LANE), lambda b, i: (b, kcol)),
                  pl.BlockSpec((seq, LANE), lambda b, i: (b, kcol + 1)),
                  pl.BlockSpec((tq, IDX_HEADS * IDX_DIM), qmap),
                  pl.BlockSpec((seq, LANE), lambda b, i: (b, icol)),
                  pl.BlockSpec((tq, LANE), qmap)],
        out_specs=pl.BlockSpec((tq, hq), qmap),
        out_shape=jax.ShapeDtypeStruct((batch * seq, hq), BF16),
        scratch_shapes=[pltpu.VMEM((seq // BIT_GROUP, 32, 8, tq), jnp.int32), pltpu.VMEM((seq, tq), F32)],
        compiler_params=_cparams(("parallel", "parallel"), 48),
        name="dsa_attention",
    )(qkv_a, qkv_a, qkv_a, qk_i, qk_i, w_i)


def _band_bias_kernel(rel_ref, o_ref):
    h = pl.program_id(0)
    i = lax.broadcasted_iota(jnp.int32, (BAND_Q, BAND_Q), 0)
    jj = lax.broadcasted_iota(jnp.int32, (BAND_Q, BAND_Q), 1)
    for jb in range(BAND_KB):
        j = jb * BAND_Q + jj
        idx = jnp.clip(i - j + LEFT_CHUNKS * CHUNK, -(CHUNK - 1), REL_CLIP) + (CHUNK - 1)
        cq = i >> CHUNK_BITS
        ck = j >> CHUNK_BITS
        d_lo = LEFT_CHUNKS * CHUNK - (jb + 1) * BAND_Q + 1
        d_hi = LEFT_CHUNKS * CHUNK - jb * BAND_Q + BAND_Q - 1
        r_lo = min(max(d_lo, -(CHUNK - 1)), REL_CLIP) + (CHUNK - 1)
        r_hi = min(max(d_hi, -(CHUNK - 1)), REL_CLIP) + (CHUNK - 1)
        val = lax.fori_loop(r_lo, r_hi + 1, lambda r, acc: jnp.where(idx == r, rel_ref[h, r], acc),
                            jnp.zeros((BAND_Q, BAND_Q), F32))
        val = jnp.where(ck >= cq, val, NEG_INF)
        o_ref[0, jb] = jnp.where(ck <= cq + LEFT_CHUNKS, val, NEG_INF)


def _band_bias(rel_bias):
    nh = rel_bias.shape[0]
    return pl.pallas_call(
        _band_bias_kernel,
        grid=(nh,),
        in_specs=[pl.BlockSpec(memory_space=pltpu.SMEM)],
        out_specs=pl.BlockSpec((1, BAND_KB, BAND_Q, BAND_Q), lambda h: (h, 0, 0, 0)),
        out_shape=jax.ShapeDtypeStruct((nh, BAND_KB, BAND_Q, BAND_Q), F32),
        name="band_bias",
    )(rel_bias)


def _band_kernel(q_ref, k_ref, v_ref, bias_ref, o_ref):
    p = pl.program_id(1)
    first = BAND_KB - 1
    shift = jnp.maximum(first - p, 0)
    s0 = pl.multiple_of(jnp.maximum(p - first, 0) * BAND_Q, BAND_Q)
    heads = [slice(h * HEAD_DIM, (h + 1) * HEAD_DIM) for h in range(N_HEADS_B)]
    logits = [_dot_nt(q_ref[:, cols], k_ref[pl.ds(s0, BAND_W), cols]) * SCALE for cols in heads]
    probs, sums = [], []
    for h, lg in enumerate(logits):
        blocks = []
        for jb in range(BAND_KB):
            src = jb + shift
            blk = lg[:, jb * BAND_Q:(jb + 1) * BAND_Q] + bias_ref[h, jnp.minimum(src, first)]
            blocks.append(jnp.where(src <= first, blk, NEG_INF))
        lg = jnp.concatenate(blocks, axis=1)
        e = jnp.exp(lg - jnp.max(lg, axis=1, keepdims=True))
        sums.append(jnp.sum(e, axis=1, keepdims=True))
        probs.append(e.astype(BF16))
    outs = [_dot(e, v_ref[pl.ds(s0, BAND_W), cols]) for e, cols in zip(probs, heads)]
    for o, s, cols in zip(outs, sums, heads):
        o_ref[:, cols] = (o / s).astype(BF16)


def _band_attention(qkv_b, bias, batch, seq):
    nq = seq // BAND_Q
    hq = N_HEADS_B * HEAD_DIM
    qmap = lambda b, p: (b * nq + p, 0)
    return pl.pallas_call(
        _band_kernel,
        grid=(batch, nq),
        in_specs=[pl.BlockSpec((BAND_Q, hq), qmap),
                  pl.BlockSpec((seq, hq), lambda b, p: (b, 1)),
                  pl.BlockSpec((seq, hq), lambda b, p: (b, 2)),
                  pl.BlockSpec(bias.shape, lambda b, p: (0, 0, 0, 0))],
        out_specs=pl.BlockSpec((BAND_Q, hq), qmap),
        out_shape=jax.ShapeDtypeStruct((batch * seq, hq), BF16),
        compiler_params=_cparams(("parallel", "parallel"), 48),
        name="band_attention",
    )(qkv_b, qkv_b, qkv_b, bias)


def _sb_blocks(qs, k_blks, v_blks, later, carry, past):
    zs = [_dot_nt(q, k) * (SCALE * LOG2_E) for q, k in zip(qs, k_blks)]
    log_betas, drops, splits = [], [], []
    for z2 in zs:
        neg_abs = lax.bitcast_convert_type(lax.bitcast_convert_type(z2, jnp.int32) | INT_MIN, F32)
        drop = jnp.maximum(z2, 0.0) + jnp.log2(1.0 + jnp.exp2(neg_abs))
        log_betas.append(z2 - drop)
        if past is not None:
            drop = jnp.where(past, drop, 0.0)
        drops.append(drop)
        hi = drop.astype(BF16)
        r1 = drop - hi.astype(F32)
        mid = r1.astype(BF16)
        lo = (r1 - mid.astype(F32)).astype(BF16)
        splits.append(jnp.concatenate([hi, mid, lo], axis=1))
    afters = [_dot(s, later) for s in splits]
    probs, tails = [], []
    for h, (log_beta, drop, after) in enumerate(zip(log_betas, drops, afters)):
        row_sum = jnp.sum(drop, axis=1, keepdims=True)
        if carry is None:
            a = jnp.exp2(log_beta - after)
            tails.append(row_sum)
        else:
            a = jnp.exp2(log_beta - carry[2 * h] - after)
            tails.append(carry[2 * h] + row_sum)
        if past is not None:
            a = jnp.where(past, a, 0.0)
        probs.append(a.astype(BF16))
    pvs = [_dot(a, v) for a, v in zip(probs, v_blks)]
    out = []
    for h, (tail, pv) in enumerate(zip(tails, pvs)):
        out += [tail, pv if carry is None else carry[2 * h + 1] + pv]
    return tuple(out)


def _sb_kernel(q_ref, k_ref, v_ref, o_ref, *, tq, nh):
    i = pl.program_id(2)
    r3 = lax.broadcasted_iota(jnp.int32, (3 * tq, tq), 0)
    c3 = lax.broadcasted_iota(jnp.int32, (3 * tq, tq), 1)
    later = jnp.where((r3 & (tq - 1)) > c3, 1.0, 0.0).astype(BF16)
    row = lax.broadcasted_iota(jnp.int32, (tq, tq), 0)
    col = lax.broadcasted_iota(jnp.int32, (tq, tq), 1)
    heads = [slice(h * HEAD_DIM, (h + 1) * HEAD_DIM) for h in range(nh)]

    def walk(s0, carry, past):
        return _sb_blocks([q_ref[:, cols] for cols in heads],
                          [k_ref[pl.ds(s0, tq), cols] for cols in heads],
                          [v_ref[pl.ds(s0, tq), cols] for cols in heads], later, carry, past)

    def least_tail(c):
        m = c[0]
        for h in range(1, nh):
            m = jnp.minimum(m, c[2 * h])
        return jnp.min(m)

    def alive(state):
        return jnp.logical_and(state[0] < i, state[1] < SB_DEAD_BITS)

    def step(state):
        c = walk(pl.multiple_of((i - 1 - state[0]) * tq, tq), state[2:], None)
        return (state[0] + 1, least_tail(c)) + c

    carry = walk(pl.multiple_of(i * tq, tq), None, col < row)
    carry = lax.while_loop(alive, step, (jnp.int32(0), least_tail(carry)) + carry)[2:]
    for h, cols in enumerate(heads):
        o_ref[:, cols] = carry[2 * h + 1].astype(BF16)


def _stick_breaking(qkv, batch, seq, tq, nh):
    nq = seq // tq
    ng = N_HEADS_C // nh
    w = nh * HEAD_DIM
    qmap = lambda b, g, i: (b * nq + i, g)
    return pl.pallas_call(
        functools.partial(_sb_kernel, tq=tq, nh=nh),
        grid=(batch, ng, nq),
        in_specs=[pl.BlockSpec((tq, w), qmap),
                  pl.BlockSpec((seq, w), lambda b, g, i: (b, ng + g)),
                  pl.BlockSpec((seq, w), lambda b, g, i: (b, 2 * ng + g))],
        out_specs=pl.BlockSpec((tq, w), qmap),
        out_shape=jax.ShapeDtypeStruct((batch * seq, N_HEADS_C * HEAD_DIM), BF16),
        compiler_params=_cparams(("parallel", "parallel", "parallel"), 32),
        name="stick_breaking",
    )(qkv, qkv, qkv)


def _xattn_kernel(xb_ref, xf_ref, kv_ref, wq_ref, wo_ref, g_ref, b_ref, of_ref, wqb_ref, wob_ref):
    @pl.when(pl.program_id(0) == 0)
    def _():
        wqb_ref[...] = wq_ref[...].astype(BF16)
        wob_ref[...] = wo_ref[...].astype(BF16)

    hk = N_HEADS_X * HEAD_DIM
    heads = [slice(h * HEAD_DIM, (h + 1) * HEAD_DIM) for h in range(N_HEADS_X)]
    for r in range(0, of_ref.shape[0], LN_SUB_ROWS):
        rows = slice(r, r + LN_SUB_ROWS)
        q = _dot(xb_ref[rows, :], wqb_ref[...]).astype(BF16)
        logits = [_dot_nt(q[:, cols], kv_ref[:, cols]) * SCALE for cols in heads]
        probs, sums = [], []
        for lg in logits:
            e = jnp.exp(lg - jnp.max(lg, axis=1, keepdims=True))
            sums.append(jnp.sum(e, axis=1, keepdims=True))
            probs.append(e.astype(BF16))
        outs = [_dot(e, kv_ref[:, hk + h * HEAD_DIM:hk + (h + 1) * HEAD_DIM]) for h, e in enumerate(probs)]
        o = jnp.concatenate([(oh / s).astype(BF16) for oh, s in zip(outs, sums)], axis=1)
        of_ref[rows, :] = _layer_norm(ALPHA * xf_ref[rows, :] + _dot(o, wob_ref[...]), g_ref[...], b_ref[...])


def _xattn(xb, xf, kv, wq, wo, lead, g, b, seq, tm):
    m, n = xf.shape
    per_seq = seq // tm
    nl = (None,) * len(lead)
    row = lambda i: (i, 0)
    const = lambda i: (0, 0)
    return pl.pallas_call(
        _xattn_kernel,
        grid=(m // tm,),
        in_specs=[pl.BlockSpec((tm, n), row), pl.BlockSpec((tm, n), row),
                  pl.BlockSpec((MEM_LEN, kv.shape[1]), lambda i: (i // per_seq, 0)),
                  pl.BlockSpec(nl + wq.shape[-2:], lambda i: lead + (0, 0)),
                  pl.BlockSpec(nl + wo.shape[-2:], lambda i: lead + (0, 0)),
                  pl.BlockSpec((1, n), const), pl.BlockSpec((1, n), const)],
        out_specs=pl.BlockSpec((tm, n), row),
        out_shape=jax.ShapeDtypeStruct((m, n), F32),
        scratch_shapes=[pltpu.VMEM(wq.shape[-2:], BF16), pltpu.VMEM(wo.shape[-2:], BF16)],
        compiler_params=_cparams(("arbitrary",), 56),
        name="xattn_ln",
    )(xb, xf, kv, wq, wo, g, b)


def _rope_tables(seq):
    pos = np.arange(seq, dtype=np.float64)[:, None]

    def half_tables(half):
        inv_freq = ROPE_THETA ** (-np.arange(half, dtype=np.float64) / half)
        ang = pos * inv_freq[None, :]
        return np.cos(ang).astype(np.float32), np.sin(ang).astype(np.float32)

    c64, s64 = half_tables(HEAD_DIM // 2)
    c32, s32 = half_tables(IDX_DIM // 2)
    rest = LANE - IDX_DIM
    ones, zeros = np.ones((seq, rest), np.float32), np.zeros((seq, rest), np.float32)
    head = (np.concatenate([c64, c64], 1), np.concatenate([-s64, s64], 1))
    idx = (np.concatenate([c32, c32, c32, c32], 1), np.concatenate([-s32, s32, -s32, s32], 1))
    idx_key = (np.concatenate([c32, c32, ones], 1), np.concatenate([-s32, s32, zeros], 1))
    return tuple(tuple(jnp.asarray(t) for t in pair) for pair in (head, idx, idx_key))


def kernel(x, mem, ln_g, ln_b, ffn_in, ffn_out, xattn_q, xattn_kv, xattn_o,
           even_in, even_out, even_rel_bias, odd_in, odd_out):
    batch, seq, d = x.shape
    t = batch * seq
    xf = x.reshape(t, d)
    memb = mem.reshape(batch * MEM_LEN, d).astype(BF16)
    tab_head, tab_idx, tab_idx_key = _rope_tables(seq)

    modes_a = ((0, HEAD_DIM),) * (N_HEADS_A + 1) + (None,)
    modes_i = ((1, IDX_DIM),) * ((IDX_HEADS * IDX_DIM) // LANE) + ((2, IDX_DIM),)

    def ln_params(layer, j):
        return ln_g[layer, j][None, :], ln_b[layer, j][None, :]

    def ffn(xf, layer, j):
        g, b = ln_params(layer, 3 * j)
        outs = _ffn(xf, ffn_in, ffn_out, (layer, j), g, b, 1024, 256, 0.5, want_bf16=(j == 0))
        return outs[0], (outs[1] if j == 0 else None)

    for layer in range(DEPTH):
        xf, xb = ffn(xf, layer, 0)

        g, b = ln_params(layer, 1)
        if layer % 2 == 0:
            w_in = even_in[layer // 2]
            w_ai = jnp.pad(w_in[:, :A_COLS + I_COLS], ((0, 0), (0, I_PAD - I_COLS))).astype(BF16)
            w_b = w_in[:, A_COLS + I_COLS:].astype(BF16)
            qkv_a, qk_i, w_idx = _proj_rope(xb, w_ai, [tab_head, tab_idx, tab_idx_key], modes_a + modes_i,
                                            (len(modes_a), len(modes_i)), seq, 512, "proj_dsa")
            qkv_b = _matmul(xb, w_b, 2048, 1024, "proj_band")
            o_a = _dsa_attention(qkv_a, qk_i, w_idx, batch, seq)
            o_b = _band_attention(qkv_b, _band_bias(even_rel_bias[layer // 2]), batch, seq)
            xf, xb = _out_ln([o_a, o_b], even_out, (layer // 2,), xf, g, b, 512, "even_out_ln")
        else:
            qkv = _matmul(xb, odd_in, 1024, 1024, "proj_odd", lead=(layer // 2,))
            o = _stick_breaking(qkv, batch, seq, 256, 4)
            xf, xb = _out_ln([o], odd_out, (layer // 2,), xf, g, b, 512, "odd_out_ln")

        g, b = ln_params(layer, 2)
        kv = _matmul(memb, xattn_kv, 1024, 1024, "proj_mem", lead=(layer,))
        xf = _xattn(xb, xf, kv, xattn_q, xattn_o, (layer,), g, b, seq, 512)

        xf, _ = ffn(xf, layer, 1)

    return xf.reshape(batch, seq, d)
```
